```python
import jax, jax.numpy as jnp
from jax import lax
import numpy as np

D_MODEL = 2048
BATCH = 4
SEQ = 2048
DEPTH = 2
DEC_BATCH = 32
DEC_SEQ = 1
PAST_LEN = 8192
PAGE_SIZE = 128

HEAD_DIM = 64
H_A = 12
G_A = 4
R_A = H_A // G_A
H_B = 12
D_A = H_A * HEAD_DIM
D_KV_A = G_A * HEAD_DIM
D_B = H_B * HEAD_DIM
D_C = D_MODEL - D_A - D_B
CONV_W = 31
CMP_LEN = 32
CMP_STRIDE = 16
SEL_BLOCK = 64
TOP_N = 16
WINDOW = 512
Q_BLOCK = 128
FORCE_BONUS = 1000.0
D_IN = D_A + 6 * D_KV_A + 3 * H_A + 3 * D_B + 2 * D_C
D_FF = -(-8 * D_MODEL // (3 * 256)) * 256
EPS = 1e-6

kernel_name = 'nsa_stickbreak_conformer_hybrid_step'


def rmsnorm(x, g):
    xf = x.astype(jnp.float32)
    y = xf * lax.rsqrt(jnp.mean(xf * xf, -1, keepdims=True) + EPS)
    return (y * g).astype(x.dtype)


def layernorm(x, g, b):
    xf = x.astype(jnp.float32)
    mu = jnp.mean(xf, -1, keepdims=True)
    var = jnp.mean(jnp.square(xf - mu), -1, keepdims=True)
    return ((xf - mu) * lax.rsqrt(var + EPS) * g + b).astype(x.dtype)


def alibi_slopes():
    h = jnp.arange(1, H_A + 1, dtype=jnp.float32)
    return (2.0 ** (-8.0 * h / H_A)).reshape(G_A, R_A)


def masked_softmax(s, mask):
    s = jnp.where(mask, s, -jnp.inf)
    m = jnp.max(s, -1, keepdims=True)
    m = jnp.where(jnp.isfinite(m), m, 0.0)
    e = jnp.where(mask, jnp.exp(s - m), 0.0)
    return e / jnp.maximum(jnp.sum(e, -1, keepdims=True), 1e-30)


def pad_rows(x, mult):
    pad = (-x.shape[1]) % mult
    return jnp.pad(x, [(0, 0), (0, pad)] + [(0, 0)] * (x.ndim - 2))


def to_blocks(a):
    B, T = a.shape[:2]
    return jnp.swapaxes(a.reshape((B, T // Q_BLOCK, Q_BLOCK) + a.shape[2:]), 0, 1)


def unblock(a):
    a = jnp.swapaxes(a, 0, 1)
    return a.reshape((a.shape[0], -1) + a.shape[3:])


def split_projection(xn, w_in, q_norm, k_norm):
    B, T, _ = xn.shape
    u = jnp.einsum('btd,de->bte', xn, w_in)
    widths = (D_A, 6 * D_KV_A, 3 * H_A, D_B, D_B, D_B)
    cuts, acc = [], 0
    for w in widths:
        acc += w
        cuts.append(acc)
    q_a, kv_a, g_a, q_b, k_b, v_b, c_in = jnp.split(u, cuts, axis=-1)
    q_a = rmsnorm(q_a.reshape(B, T, G_A, R_A, HEAD_DIM), q_norm)
    kv_a = kv_a.reshape(B, T, 6, G_A, HEAD_DIM)
    k_cmp, v_cmp = kv_a[:, :, 0], kv_a[:, :, 1]
    k_slc, v_slc = rmsnorm(kv_a[:, :, 2], k_norm[1]), kv_a[:, :, 3]
    k_win, v_win = rmsnorm(kv_a[:, :, 4], k_norm[2]), kv_a[:, :, 5]
    gates = jax.nn.sigmoid(g_a.reshape(B, T, G_A, R_A, 3))
    q_b = q_b.reshape(B, T, H_B, HEAD_DIM)
    k_b = k_b.reshape(B, T, H_B, HEAD_DIM)
    v_b = v_b.reshape(B, T, H_B, HEAD_DIM)
    return q_a, (k_cmp, v_cmp, k_slc, v_slc), (k_win, v_win), gates, (q_b, k_b, v_b), c_in


def compress(rows, pos_emb, w1, w2):
    B, T, G, D = rows.shape
    sub = rows.reshape(B, T // CMP_STRIDE, CMP_STRIDE, G, D)
    blocks = jnp.concatenate([sub[:, :-1], sub[:, 1:]], axis=2) + pos_emb[None, None, :, None, :]
    flat = jnp.moveaxis(blocks, 3, 2).reshape(B, blocks.shape[1], G, CMP_LEN * D)
    h = jax.nn.gelu(jnp.einsum('bngf,fe->bnge', flat, w1))
    return jnp.einsum('bnge,ef->bngf', h, w2)


def nsa_memory(k_cmp, v_cmp, k_slc, v_slc, cmp_pos, cmp_w1, cmp_w2, k_norm_c):
    k_cmp, v_cmp, k_slc, v_slc = (pad_rows(a, SEL_BLOCK) for a in (k_cmp, v_cmp, k_slc, v_slc))
    kc = rmsnorm(compress(k_cmp, cmp_pos[0], cmp_w1[0], cmp_w2[0]), k_norm_c)
    vc = compress(v_cmp, cmp_pos[1], cmp_w1[1], cmp_w2[1])
    B, Tp = k_slc.shape[:2]
    ks = k_slc.reshape(B, Tp // SEL_BLOCK, SEL_BLOCK, G_A, HEAD_DIM)
    vs = v_slc.reshape(B, Tp // SEL_BLOCK, SEL_BLOCK, G_A, HEAD_DIM)
    return kc, vc, ks, vs


def nsa_global(q, tq, kc, vc, ks_blk, vs_blk, slopes):
    B, Q, G, R, D = q.shape
    NC, NSEL = kc.shape[1], ks_blk.shape[1]
    scale = D ** -0.5
    tqf = tq.astype(jnp.float32)
    ends = jnp.arange(NC) * CMP_STRIDE + (CMP_LEN - 1)
    sc = jnp.einsum('bqgrd,bngd->bgrqn', q, kc, preferred_element_type=jnp.float32) * scale
    sc = sc - slopes[None, :, :, None, None] * (tqf[:, None] - ends.astype(jnp.float32))[None, None, None]
    p_c = masked_softmax(sc, (ends[None, :] <= tq[:, None])[None, None, None])
    o_c = jnp.einsum('bgrqn,bngd->bqgrd', p_c.astype(vc.dtype), vc)
    imp = p_c.sum(2)
    n_sub = NSEL * (SEL_BLOCK // CMP_STRIDE)
    imp = jnp.pad(imp, ((0, 0), (0, 0), (0, 0), (0, n_sub - NC)))
    imp = imp.reshape(B, G, Q, NSEL, SEL_BLOCK // CMP_STRIDE).sum(-1)
    blk = jnp.arange(NSEL)[None, :]
    cur = (tq // SEL_BLOCK)[:, None]
    valid = blk <= cur
    forced = (blk == 0) | (blk == cur) | (blk == cur - 1)
    score = jnp.where(valid, jnp.where(forced, FORCE_BONUS, imp), -1.0)
    n_top = min(TOP_N, NSEL)
    top_v, top_i = lax.top_k(score, n_top)
    sel_ok = top_v >= 0.0
    ks_t = jnp.moveaxis(ks_blk, 3, 1)
    vs_t = jnp.moveaxis(vs_blk, 3, 1)
    b_ix = jnp.arange(B)[:, None, None, None]
    g_ix = jnp.arange(G)[None, :, None, None]
    kg = ks_t[b_ix, g_ix, top_i]
    vg = vs_t[b_ix, g_ix, top_i]
    pos = top_i[..., None] * SEL_BLOCK + jnp.arange(SEL_BLOCK)
    ss = jnp.einsum('bqgrd,bgqnsd->bgrqns', q, kg, preferred_element_type=jnp.float32) * scale
    ss = ss - slopes[None, :, :, None, None, None] * (tqf[None, None, None, :, None, None] - pos[:, :, None].astype(jnp.float32))
    mask_s = (pos <= tq[None, None, :, None, None]) & sel_ok[..., None]
    p_s = masked_softmax(ss.reshape(B, G, R, Q, n_top * SEL_BLOCK), mask_s.reshape(B, G, 1, Q, n_top * SEL_BLOCK))
    p_s = p_s.reshape(B, G, R, Q, n_top, SEL_BLOCK)
    o_s = jnp.einsum('bgrqns,bgqnsd->bqgrd', p_s.astype(vg.dtype), vg)
    return o_c, o_s


def window_attend(q, tq, k, v, tk, slopes):
    s = jnp.einsum('bqgrd,bkgd->bgrqk', q, k, preferred_element_type=jnp.float32) * (q.shape[-1] ** -0.5)
    dist = (tq[:, None] - tk[None, :])
    s = s - slopes[None, :, :, None, None] * dist.astype(jnp.float32)[None, None, None]
    mask = (dist >= 0) & (dist < WINDOW) & (tk[None, :] >= 0)
    p = masked_softmax(s, mask[None, None, None])
    return jnp.einsum('bgrqk,bkgd->bqgrd', p.astype(v.dtype), v)


def stick_breaking(q, tq, k, v, tk):
    z = jnp.einsum('bqhd,bkhd->bhqk', q, k, preferred_element_type=jnp.float32) * (q.shape[-1] ** -0.5)
    mask = (tk[None, :] < tq[:, None])[None, None]
    log_fail = jnp.where(mask, jax.nn.log_sigmoid(-z), 0.0)
    later = lax.cumsum(log_fail, axis=3, reverse=True) - log_fail
    a = jnp.where(mask, jnp.exp(jax.nn.log_sigmoid(z) + later), 0.0)
    return jnp.einsum('bhqk,bkhd->bqhd', a.astype(v.dtype), v)


def conv_module(c_in, buf, conv_w, conv_b, ln_g, ln_b):
    a, g = jnp.split(c_in, 2, axis=-1)
    u = a * jax.nn.sigmoid(g)
    seq = jnp.concatenate([buf.astype(u.dtype), u], axis=1)
    y = lax.conv_general_dilated(seq, conv_w[:, None, :], window_strides=(1,), padding='VALID',
                                 dimension_numbers=('NWC', 'WIO', 'NWC'), feature_group_count=D_C) + conv_b
    y = jax.nn.silu(layernorm(y, ln_g, ln_b))
    return y, seq[:, -(CONV_W - 1):]


def merge(o_c, o_s, o_w, gates, o_b, conv_out, out_norm_a, out_norm_b, w_out):
    B, T = gates.shape[:2]
    o_a = (gates[..., 0:1] * o_c + gates[..., 1:2] * o_s + gates[..., 2:3] * o_w).reshape(B, T, D_A)
    h = jnp.concatenate([rmsnorm(o_a, out_norm_a), rmsnorm(o_b.reshape(B, T, D_B), out_norm_b), conv_out], -1)
    return jnp.einsum('bte,ed->btd', h, w_out)


def swiglu(x, w_gate, w_up, w_down):
    h = jax.nn.silu(jnp.einsum('btd,df->btf', x, w_gate)) * jnp.einsum('btd,df->btf', x, w_up)
    return jnp.einsum('btf,fd->btd', h, w_down)


def mixer_prompt(xn, mw, slopes):
    (w_in, w_out, q_norm, k_norm, cmp_pos, cmp_w1, cmp_w2,
     out_norm_a, out_norm_b, conv_w, conv_b, conv_ln_g, conv_ln_b) = mw
    B, T, _ = xn.shape
    q_a, glob_rows, (k_win, v_win), gates, (q_b, k_b, v_b), c_in = split_projection(xn, w_in, q_norm, k_norm)
    tpos = jnp.arange(T)
    kc, vc, ks, vs = nsa_memory(*glob_rows, cmp_pos, cmp_w1, cmp_w2, k_norm[0])
    qa_blk, t_blk = to_blocks(q_a), tpos.reshape(-1, Q_BLOCK)
    o_c, o_s = lax.map(lambda a: nsa_global(a[0], a[1], kc, vc, ks, vs, slopes), (qa_blk, t_blk))
    kwp = jnp.pad(k_win, ((0, 0), (WINDOW, 0), (0, 0), (0, 0)))
    vwp = jnp.pad(v_win, ((0, 0), (WINDOW, 0), (0, 0), (0, 0)))

    def band(a):
        qi, ti, i = a
        start = i * Q_BLOCK
        kk = lax.dynamic_slice_in_dim(kwp, start, WINDOW + Q_BLOCK, axis=1)
        vv = lax.dynamic_slice_in_dim(vwp, start, WINDOW + Q_BLOCK, axis=1)
        tk = start - WINDOW + jnp.arange(WINDOW + Q_BLOCK)
        return window_attend(qi, ti, kk, vv, tk, slopes)

    o_w = lax.map(band, (qa_blk, t_blk, jnp.arange(T // Q_BLOCK)))
    o_b = lax.map(lambda a: stick_breaking(a[0], a[1], k_b, v_b, tpos), (to_blocks(q_b), t_blk))
    conv_out, conv_state = conv_module(c_in, jnp.zeros((B, CONV_W - 1, D_C), c_in.dtype),
                                       conv_w, conv_b, conv_ln_g, conv_ln_b)
    mix = merge(unblock(o_c), unblock(o_s), unblock(o_w), gates, unblock(o_b), conv_out,
                out_norm_a, out_norm_b, w_out)
    nsa_rows = jnp.stack(glob_rows, axis=2)
    sb_rows = jnp.stack([k_b, v_b], axis=2)
    win_state = jnp.stack([k_win, v_win], axis=2)[:, -min(WINDOW, T):]
    return mix, (nsa_rows, sb_rows, win_state, conv_state)


def mixer_sample(xn, mw, slopes, past_nsa, past_sb, win_buf, conv_buf):
    (w_in, w_out, q_norm, k_norm, cmp_pos, cmp_w1, cmp_w2,
     out_norm_a, out_norm_b, conv_w, conv_b, conv_ln_g, conv_ln_b) = mw
    B, S, _ = xn.shape
    P = past_nsa.shape[1]
    WB = win_buf.shape[1]
    q_a, glob_rows, (k_win, v_win), gates, (q_b, k_b, v_b), c_in = split_projection(xn, w_in, q_norm, k_norm)
    tq = P + jnp.arange(S)
    nsa_rows = jnp.stack(glob_rows, axis=2)
    nsa_all = jnp.concatenate([past_nsa, nsa_rows], axis=1)
    kc, vc, ks, vs = nsa_memory(nsa_all[:, :, 0], nsa_all[:, :, 1], nsa_all[:, :, 2], nsa_all[:, :, 3],
                                cmp_pos, cmp_w1, cmp_w2, k_norm[0])
    o_c, o_s = nsa_global(q_a, tq, kc, vc, ks, vs, slopes)
    win_all = jnp.concatenate([win_buf, jnp.stack([k_win, v_win], axis=2)], axis=1)
    tk_w = jnp.concatenate([P - WB + jnp.arange(WB), tq])
    o_w = window_attend(q_a, tq, win_all[:, :, 0], win_all[:, :, 1], tk_w, slopes)
    sb_rows = jnp.stack([k_b, v_b], axis=2)
    sb_all = jnp.concatenate([past_sb, sb_rows], axis=1)
    o_b = stick_breaking(q_b, tq, sb_all[:, :, 0], sb_all[:, :, 1], jnp.arange(P + S))
    conv_out, conv_state = conv_module(c_in, conv_buf, conv_w, conv_b, conv_ln_g, conv_ln_b)
    mix = merge(o_c, o_s, o_w, gates, o_b, conv_out, out_norm_a, out_norm_b, w_out)
    return mix, (nsa_rows, sb_rows, win_all[:, -WB:], conv_state)


def setup_inputs(seed: int = 0) -> dict:
    key = jax.random.key(seed)
    ks = jax.random.split(key, 32)
    n_pages = PAST_LEN // PAGE_SIZE
    n_used = DEC_BATCH * n_pages
    n_pool = n_used + n_used // 4
    win_buf = min(WINDOW, PAST_LEN)

    def nrm(k, shape, scale):
        return scale * jax.random.normal(k, shape, jnp.float32)

    def gain(k, shape):
        return 1.0 + nrm(k, shape, 0.01)

    page_table = jax.random.permutation(ks[7], n_pool)[:n_used].reshape(DEC_BATCH, n_pages).astype(jnp.int32)
    return {
        'x_prompt': nrm(ks[0], (BATCH, SEQ, D_MODEL), 1.0),
        'x_sample': nrm(ks[1], (DEC_BATCH, DEC_SEQ, D_MODEL), 1.0),
        'cache_nsa': nrm(ks[2], (DEPTH, n_pool, PAGE_SIZE, 4, G_A, HEAD_DIM), 1.0),
        'cache_sb': nrm(ks[3], (DEPTH, n_pool, PAGE_SIZE, 2, H_B, HEAD_DIM), 1.0),
        'state_win': nrm(ks[4], (DEPTH, DEC_BATCH, win_buf, 2, G_A, HEAD_DIM), 1.0),
        'state_conv': nrm(ks[5], (DEPTH, DEC_BATCH, CONV_W - 1, D_C), 0.5),
        'page_table': page_table,
        'w_in': nrm(ks[8], (DEPTH, D_MODEL, D_IN), D_MODEL ** -0.5),
        'w_out': nrm(ks[9], (DEPTH, D_MODEL, D_MODEL), D_MODEL ** -0.5),
        'norm_mix': gain(ks[10], (DEPTH, D_MODEL)),
        'norm_ffn': gain(ks[11], (DEPTH, D_MODEL)),
        'q_norm': gain(ks[12], (DEPTH, HEAD_DIM)),
        'k_norm': gain(ks[13], (DEPTH, 3, HEAD_DIM)),
        'cmp_pos': nrm(ks[14], (DEPTH, 2, CMP_LEN, HEAD_DIM), 0.1),
        'cmp_w1': nrm(ks[15], (DEPTH, 2, CMP_LEN * HEAD_DIM, HEAD_DIM), (CMP_LEN * HEAD_DIM) ** -0.5),
        'cmp_w2': nrm(ks[16], (DEPTH, 2, HEAD_DIM, HEAD_DIM), HEAD_DIM ** -0.5),
        'out_norm_a': gain(ks[17], (DEPTH, D_A)),
        'out_norm_b': gain(ks[18], (DEPTH, D_B)),
        'conv_w': nrm(ks[19], (DEPTH, CONV_W, D_C), CONV_W ** -0.5),
        'conv_b': nrm(ks[20], (DEPTH, D_C), 0.01),
        'conv_ln_g': gain(ks[21], (DEPTH, D_C)),
        'conv_ln_b': nrm(ks[22], (DEPTH, D_C), 0.01),
        'w_gate': nrm(ks[23], (DEPTH, D_MODEL, D_FF), D_MODEL ** -0.5),
        'w_up': nrm(ks[24], (DEPTH, D_MODEL, D_FF), D_MODEL ** -0.5),
        'w_down': nrm(ks[25], (DEPTH, D_FF, D_MODEL), D_FF ** -0.5),
    }


def reference(x_prompt, x_sample, cache_nsa, cache_sb, state_win, state_conv, page_table,
              w_in, w_out, norm_mix, norm_ffn, q_norm, k_norm, cmp_pos, cmp_w1, cmp_w2,
              out_norm_a, out_norm_b, conv_w, conv_b, conv_ln_g, conv_ln_b, w_gate, w_up, w_down):
    slopes = alibi_slopes()
    n_seq = page_table.shape[0]
    xp, xs = x_prompt, x_sample
    p_nsa, p_sb, p_win, p_conv = [], [], [], []
    s_nsa, s_sb, s_win, s_conv = [], [], [], []
    for l in range(DEPTH):
        mw = (w_in[l], w_out[l], q_norm[l], k_norm[l], cmp_pos[l], cmp_w1[l], cmp_w2[l],
              out_norm_a[l], out_norm_b[l], conv_w[l], conv_b[l], conv_ln_g[l], conv_ln_b[l])
        mix, st = mixer_prompt(rmsnorm(xp, norm_mix[l]), mw, slopes)
        xp = xp + mix
        xp = xp + swiglu(rmsnorm(xp, norm_ffn[l]), w_gate[l], w_up[l], w_down[l])
        p_nsa.append(st[0]); p_sb.append(st[1]); p_win.append(st[2]); p_conv.append(st[3])
        past_nsa = cache_nsa[l][page_table].reshape((n_seq, -1) + cache_nsa.shape[3:])
        past_sb = cache_sb[l][page_table].reshape((n_seq, -1) + cache_sb.shape[3:])
        mix, st = mixer_sample(rmsnorm(xs, norm_mix[l]), mw, slopes, past_nsa, past_sb, state_win[l], state_conv[l])
        xs = xs + mix
        xs = xs + swiglu(rmsnorm(xs, norm_ffn[l]), w_gate[l], w_up[l], w_down[l])
        s_nsa.append(st[0]); s_sb.append(st[1]); s_win.append(st[2]); s_conv.append(st[3])
    y_prompt, y_sample = xp, xs
    nsa_rows_prompt = jnp.stack(p_nsa)
    sb_rows_prompt = jnp.stack(p_sb)
    win_prompt = jnp.stack(p_win)
    conv_prompt = jnp.stack(p_conv)
    nsa_rows_sample = jnp.stack(s_nsa)
    sb_rows_sample = jnp.stack(s_sb)
    win_sample = jnp.stack(s_win)
    conv_sample = jnp.stack(s_conv)
    return (y_prompt, y_sample, nsa_rows_prompt, sb_rows_prompt, win_prompt, conv_prompt,
            nsa_rows_sample, sb_rows_sample, win_sample, conv_sample)
```

```python
import functools

import numpy as np
import jax
import jax.numpy as jnp
from jax import lax
from jax.experimental import pallas as pl
from jax.experimental.pallas import tpu as pltpu

F32 = jnp.float32
BF16 = jnp.bfloat16

LANE = 128
SUBLANE = 8
MXU_DIM = 256
VMEM_LIMIT_BYTES = 56 * 1024 * 1024

D_MODEL = 2048
DEPTH = 2
HEAD_DIM = 64
H_A = 12
G_A = 4
R_A = H_A // G_A
H_B = 12
D_A = H_A * HEAD_DIM
D_KV_A = G_A * HEAD_DIM
D_B = H_B * HEAD_DIM
D_C = D_MODEL - D_A - D_B
CONV_W = 31
CMP_LEN = 32
CMP_STRIDE = 16
SEL_BLOCK = 64
TOP_N = 16
WINDOW = 512
FORCE_BONUS = 1000.0
N_GATE = 3 * H_A
D_IN = D_A + 6 * D_KV_A + N_GATE + 3 * D_B + 2 * D_C
D_FF = -(-8 * D_MODEL // (3 * 256)) * 256
EPS = 1e-6
QK_SCALE = HEAD_DIM ** -0.5
NEG_BIG = -1e30

C_QA = 0
C_KV = C_QA + D_A
C_QB = C_KV + 6 * D_KV_A
C_KB = C_QB + D_B
C_VB = C_KB + D_B
C_CIN = C_VB + D_B
C_GATE = C_CIN + 2 * D_C
D_IN_PAD = C_GATE + LANE

SLOPES = (2.0 ** (-8.0 * np.arange(1, H_A + 1, dtype=np.float32) / H_A)).astype(np.float32).reshape(G_A, R_A)


def _cparams(sem):
    return pltpu.CompilerParams(dimension_semantics=sem, vmem_limit_bytes=VMEM_LIMIT_BYTES)


def _const_spec(shape):
    nd = len(shape)
    return pl.BlockSpec(shape, lambda *_: (0,) * nd)


def _dot(a, b):
    return jnp.dot(a, b, preferred_element_type=F32)


def _dot_nt(a, b):
    return lax.dot_general(a, b, (((1,), (1,)), ((), ())), preferred_element_type=F32)


def _split_dot(x, w_bf16):
    hi = x.astype(BF16)
    lo = (x - hi.astype(F32)).astype(BF16)
    return _dot(hi, w_bf16) + _dot(lo, w_bf16)


def _sigmoid(x):
    return 1.0 / (1.0 + jnp.exp(-x))


def _rmsnorm_rows(x, gain):
    ms = jnp.mean(x * x, axis=-1, keepdims=True)
    return x * lax.rsqrt(ms + EPS) * gain


def _head_rmsnorm(u, gain_tile, pmat):
    outs = []
    for c in range(u.shape[1] // MXU_DIM):
        uc = u[:, c * MXU_DIM:(c + 1) * MXU_DIM]
        ms = _split_dot(uc * uc, pmat)
        outs.append(uc * lax.rsqrt(ms + EPS) * gain_tile)
    return outs[0] if len(outs) == 1 else jnp.concatenate(outs, axis=-1)


def _masked_softmax(s, mask):
    s = jnp.where(mask, s, -jnp.inf)
    m = jnp.max(s, axis=-1, keepdims=True)
    m = jnp.where(m == -jnp.inf, 0.0, m)
    e = jnp.where(mask, jnp.exp(s - m), 0.0)
    return e / jnp.maximum(jnp.sum(e, axis=-1, keepdims=True), 1e-30)


def _log_sigmoid_neg(z):
    return -(jnp.maximum(z, 0.0) + jnp.log(1.0 + jnp.exp(-jnp.abs(z))))


def _gelu_tanh(x):
    return 0.5 * x * (1.0 + jnp.tanh(np.float32(np.sqrt(2.0 / np.pi)) * (x + 0.044715 * (x * x * x))))


def _in_proj_kernel(x_ref, gn_ref, w_ref, qn_ref, kn1_ref, kn2_ref, pmat_ref,
                    qa_ref, nsa_ref, win_ref, qb_ref, sb_ref, cin_ref, gate_ref):
    xn = _rmsnorm_rows(x_ref[...], gn_ref[...]).astype(BF16)
    pmat = pmat_ref[...]

    def proj(lo, hi):
        return _dot(xn, w_ref[:, lo:hi])

    qa_ref[...] = _head_rmsnorm(proj(C_QA, C_KV), qn_ref[...], pmat)
    kv = proj(C_KV, C_QB)
    nsa_ref[:, 0:2 * D_KV_A] = kv[:, 0:2 * D_KV_A]
    nsa_ref[:, 2 * D_KV_A:3 * D_KV_A] = _head_rmsnorm(kv[:, 2 * D_KV_A:3 * D_KV_A], kn1_ref[...], pmat)
    nsa_ref[:, 3 * D_KV_A:4 * D_KV_A] = kv[:, 3 * D_KV_A:4 * D_KV_A]
    win_ref[:, 0:D_KV_A] = _head_rmsnorm(kv[:, 4 * D_KV_A:5 * D_KV_A], kn2_ref[...], pmat)
    win_ref[:, D_KV_A:2 * D_KV_A] = kv[:, 5 * D_KV_A:6 * D_KV_A]
    qb_ref[...] = proj(C_QB, C_KB)
    sb_ref[...] = proj(C_KB, C_CIN)
    cin_ref[...] = proj(C_CIN, C_GATE)
    gate_ref[...] = _sigmoid(proj(C_GATE, D_IN_PAD))


def _in_proj(x2d, gn, w_in_p, qn4, kn1_4, kn2_4, pmat, tm):
    m = x2d.shape[0]
    row = lambda w: pl.BlockSpec((tm, w), lambda i: (i, 0))
    widths = (D_A, 4 * D_KV_A, 2 * D_KV_A, D_B, 2 * D_B, 2 * D_C, LANE)
    return pl.pallas_call(
        _in_proj_kernel,
        grid=(m // tm,),
        in_specs=[row(D_MODEL), _const_spec((1, D_MODEL)),
                  pl.BlockSpec((D_MODEL, D_IN_PAD), lambda i: (0, 0), pipeline_mode=pl.Buffered(1)),
                  _const_spec((1, MXU_DIM)), _const_spec((1, MXU_DIM)), _const_spec((1, MXU_DIM)),
                  _const_spec((MXU_DIM, MXU_DIM))],
        out_specs=[row(w) for w in widths],
        out_shape=[jax.ShapeDtypeStruct((m, w), F32) for w in widths],
        compiler_params=_cparams(("arbitrary",)),
        name="in_proj",
    )(x2d, gn, w_in_p, qn4, kn1_4, kn2_4, pmat)


def _out_proj_kernel(oa_ref, ob_ref, cv_ref, x_ref, ga_ref, gb_ref, w_ref, y_ref):
    ha = _rmsnorm_rows(oa_ref[...], ga_ref[...]).astype(BF16)
    hb = _rmsnorm_rows(ob_ref[...], gb_ref[...]).astype(BF16)
    hc = cv_ref[...].astype(BF16)
    acc = _dot(ha, w_ref[0:D_A, :]) + _dot(hb, w_ref[D_A:D_A + D_B, :]) + _dot(hc, w_ref[D_A + D_B:, :])
    y_ref[...] = x_ref[...] + acc


def _out_proj(oa, ob, cv, x2d, ga, gb, w_out_p, tm):
    m = x2d.shape[0]
    row = lambda w: pl.BlockSpec((tm, w), lambda i: (i, 0))
    return pl.pallas_call(
        _out_proj_kernel,
        grid=(m // tm,),
        in_specs=[row(D_A), row(D_B), row(D_C), row(D_MODEL), _const_spec((1, D_A)), _const_spec((1, D_B)),
                  _const_spec((D_MODEL, D_MODEL))],
        out_specs=row(D_MODEL),
        out_shape=jax.ShapeDtypeStruct((m, D_MODEL), F32),
        compiler_params=_cparams(("arbitrary",)),
        name="out_proj",
    )(oa, ob, cv, x2d, ga, gb, w_out_p)


def _ffn_up_kernel(x_ref, gn_ref, wg_ref, wu_ref, h_ref, xn_ref):
    @pl.when(pl.program_id(1) == 0)
    def _():
        xn_ref[...] = _rmsnorm_rows(x_ref[...], gn_ref[...]).astype(BF16)

    xn = xn_ref[...]
    a = _dot(xn, wg_ref[...])
    b = _dot(xn, wu_ref[...])
    h_ref[...] = (a * _sigmoid(a) * b).astype(BF16)


def _ffn_down_kernel(h_ref, x_ref, wd_ref, y_ref):
    y_ref[...] = x_ref[...] + _dot(h_ref[...], wd_ref[...])


def _ffn(x2d, gn, wg, wu, wd, tm, tf, tn):
    m = x2d.shape[0]
    h = pl.pallas_call(
        _ffn_up_kernel,
        grid=(m // tm, D_FF // tf),
        in_specs=[pl.BlockSpec((tm, D_MODEL), lambda i, j: (i, 0)), _const_spec((1, D_MODEL)),
                  pl.BlockSpec((D_MODEL, tf), lambda i, j: (0, j)), pl.BlockSpec((D_MODEL, tf), lambda i, j: (0, j))],
        out_specs=pl.BlockSpec((tm, tf), lambda i, j: (i, j)),
        out_shape=jax.ShapeDtypeStruct((m, D_FF), BF16),
        scratch_shapes=[pltpu.VMEM((tm, D_MODEL), BF16)],
        compiler_params=_cparams(("arbitrary", "arbitrary")),
        name="ffn_up",
    )(x2d, gn, wg, wu)
    return pl.pallas_call(
        _ffn_down_kernel,
        grid=(m // tm, D_MODEL // tn),
        in_specs=[pl.BlockSpec((tm, D_FF), lambda i, j: (i, 0)), pl.BlockSpec((tm, tn), lambda i, j: (i, j)),
                  pl.BlockSpec((D_FF, tn), lambda i, j: (0, j))],
        out_specs=pl.BlockSpec((tm, tn), lambda i, j: (i, j)),
        out_shape=jax.ShapeDtypeStruct((m, D_MODEL), F32),
        compiler_params=_cparams(("arbitrary", "arbitrary")),
        name="ffn_down",
    )(h, x2d, wd)


def _compress_rows(get_piece, n_rows, pos_ref, w1_ref, w2_ref, kn0_ref, pmat_ref):
    outs = []
    for kv in range(2):
        first = jnp.zeros((n_rows, MXU_DIM), F32)
        second = jnp.zeros((n_rows, MXU_DIM), F32)
        for p in range(CMP_STRIDE):
            xp = get_piece(p, kv)
            first = first + _dot((xp + pos_ref[kv, p:p + 1, :]).astype(BF16), w1_ref[kv, p])
            q = CMP_STRIDE + p
            second = second + _dot((xp + pos_ref[kv, q:q + 1, :]).astype(BF16), w1_ref[kv, q])
        pre = first + pltpu.roll(second, n_rows - 1, axis=0)
        o = _dot(_gelu_tanh(pre).astype(BF16), w2_ref[kv])
        if kv == 0:
            o = _head_rmsnorm(o, kn0_ref[...], pmat_ref[...])
        outs.append(o)
    return outs


def _compress_prompt_kernel(x_ref, pos_ref, w1_ref, w2_ref, kn0_ref, pmat_ref, kc_ref, vc_ref):
    n_rows = x_ref.shape[0]
    row_w = 4 * D_KV_A

    def piece(p, kv):
        lo = p * row_w + kv * D_KV_A
        return x_ref[:, lo:lo + D_KV_A]

    kc, vc = _compress_rows(piece, n_rows, pos_ref, w1_ref, w2_ref, kn0_ref, pmat_ref)
    kc_ref[...] = kc
    vc_ref[...] = vc


def _compress_prompt(nsa16, pos4, w1bd, w2bd, kn0_4, pmat):
    b, nsub, w = nsa16.shape
    return pl.pallas_call(
        _compress_prompt_kernel,
        grid=(b,),
        in_specs=[pl.BlockSpec((None, nsub, w), lambda i: (i, 0, 0)),
                  _const_spec(pos4.shape), _const_spec(w1bd.shape), _const_spec(w2bd.shape),
                  _const_spec((1, MXU_DIM)), _const_spec((MXU_DIM, MXU_DIM))],
        out_specs=[pl.BlockSpec((None, nsub, D_KV_A), lambda i: (i, 0, 0))] * 2,
        out_shape=[jax.ShapeDtypeStruct((b, nsub, D_KV_A), F32)] * 2,
        compiler_params=_cparams(("arbitrary",)),
        name="compress_prompt",
    )(nsa16, pos4, w1bd, w2bd, kn0_4, pmat)


CMP_PAGES_PER_STEP = 16


def _compress_sample_kernel(pt_ref, *refs):
    npg = CMP_PAGES_PER_STEP
    page_refs = refs[:npg + 1]
    pos_ref, w1_ref, w2_ref, kn0_ref, pmat_ref, kc_ref, vc_ref = refs[npg + 1:]
    row_w = 4 * D_KV_A
    n_rows = (npg + 1) * SUBLANE

    def piece(p, kv):
        lo = p * row_w + kv * D_KV_A
        return jnp.concatenate([r[:, lo:lo + D_KV_A] for r in page_refs], axis=0)

    kc, vc = _compress_rows(piece, n_rows, pos_ref, w1_ref, w2_ref, kn0_ref, pmat_ref)
    kc_ref[...] = kc[0:npg * SUBLANE]
    vc_ref[...] = vc[0:npg * SUBLANE]


def _compress_sample(page_table, cache16, pos4, w1bd, w2bd, kn0_4, pmat):
    n_seq, n_pages = page_table.shape
    npg = CMP_PAGES_PER_STEP
    n_steps = n_pages // npg
    w = cache16.shape[-1]

    def page_spec(k):
        def imap(b, j, pt):
            return (pt[b, jnp.minimum(j * npg + k, n_pages - 1)], 0, 0)
        return pl.BlockSpec((None, SUBLANE, w), imap)

    const = lambda shape: pl.BlockSpec(shape, lambda b, j, pt: (0,) * len(shape))
    out_spec = pl.BlockSpec((None, npg * SUBLANE, D_KV_A), lambda b, j, pt: (b, j, 0))
    grid_spec = pltpu.PrefetchScalarGridSpec(
        num_scalar_prefetch=1,
        grid=(n_seq, n_steps),
        in_specs=[page_spec(k) for k in range(npg + 1)]
        + [const(pos4.shape), const(w1bd.shape), const(w2bd.shape), const((1, MXU_DIM)), const((MXU_DIM, MXU_DIM))],
        out_specs=[out_spec, out_spec],
    )
    return pl.pallas_call(
        _compress_sample_kernel,
        grid_spec=grid_spec,
        out_shape=[jax.ShapeDtypeStruct((n_seq, n_pages * SUBLANE, D_KV_A), F32)] * 2,
        compiler_params=_cparams(("arbitrary", "arbitrary")),
        name="compress_sample",
    )(page_table, *([cache16] * (npg + 1)), pos4, w1bd, w2bd, kn0_4, pmat)


def _select_blocks(score, n_top):
    lane = lax.broadcasted_iota(jnp.int32, score.shape, 1)
    sel = jnp.zeros(score.shape, F32)
    big = jnp.int32(2 ** 30)
    for _ in range(n_top):
        m = jnp.max(score, axis=-1, keepdims=True)
        idx = jnp.min(jnp.where(score == m, lane, big), axis=-1, keepdims=True)
        pick = lane == idx
        sel = jnp.where(pick & (m >= 0.0), 1.0, sel)
        score = jnp.where(pick, -jnp.inf, score)
    return sel


def _block_scores(imp, tq_col, n_sel):
    width = imp.shape[1]
    per = SEL_BLOCK // CMP_STRIDE
    pooled = imp
    for k in range(1, per):
        pooled = pooled + pltpu.roll(imp, width - k, axis=1)
    lane = lax.broadcasted_iota(jnp.int32, imp.shape, 1)
    blk = lax.shift_right_logical(lane, 2)
    cur = lax.shift_right_logical(tq_col, 6)
    valid = blk <= cur
    forced = (blk == 0) | (blk == cur) | (blk == cur - 1)
    score = jnp.where(valid, jnp.where(forced, FORCE_BONUS, pooled), -1.0)
    cand = ((lane & (per - 1)) == 0) & (blk < n_sel)
    return jnp.where(cand, score, -jnp.inf)


Q_TILE = 128
SEL_CHUNK = 256


def _nsa_prompt_kernel(qa_ref, gate_ref, kc_ref, vc_ref, ks_ref, vs_ref, kw_ref, vw_ref, place_ref, expand_ref,
                       out_ref, km_ref, m_ref, l_ref, acc_ref):
    t_len = ks_ref.shape[0]
    n_cmp = kc_ref.shape[0]
    n_sel = t_len // SEL_BLOCK
    qi = pl.program_id(1)
    start = qi * Q_TILE
    rows3 = R_A * Q_TILE
    rowi = lax.broadcasted_iota(jnp.int32, (rows3, 1), 0)
    r_of_row = lax.shift_right_logical(rowi, 7)
    tq3 = start + (rowi & (Q_TILE - 1))
    tq3f = tq3.astype(F32)
    tq1 = tq3[0:Q_TILE]
    lane_out = lax.broadcasted_iota(jnp.int32, (Q_TILE, MXU_DIM), 1)
    n_chunks = lax.shift_right_logical(start + Q_TILE + SEL_CHUNK - 1, 8)
    outs = [jnp.zeros((Q_TILE, MXU_DIM), F32) for _ in range(R_A)]

    for g in range(G_A):
        qs = []
        for r in range(R_A):
            h = R_A * g + r
            qblk = qa_ref[:, (h // 2) * LANE:(h // 2 + 1) * LANE].astype(BF16)
            qs.append(_dot(qblk, place_ref[h]) * QK_SCALE)
        q3 = jnp.concatenate(qs, axis=0).astype(BF16)
        slope3 = jnp.where(r_of_row == 0, float(SLOPES[g, 0]),
                           jnp.where(r_of_row == 1, float(SLOPES[g, 1]), float(SLOPES[g, 2])))

        s = _dot_nt(q3, kc_ref[...].astype(BF16))
        ends = lax.broadcasted_iota(jnp.int32, (1, n_cmp), 1) * CMP_STRIDE + (CMP_LEN - 1)
        s = s - slope3 * (tq3f - ends.astype(F32))
        p_c = _masked_softmax(s, ends <= tq3)
        o_c = _dot(p_c.astype(BF16), vc_ref[...].astype(BF16))

        imp = p_c[0:Q_TILE] + p_c[Q_TILE:2 * Q_TILE] + p_c[2 * Q_TILE:3 * Q_TILE]
        sel = _select_blocks(_block_scores(imp, tq1, n_sel), min(TOP_N, n_sel))
        km = _dot(sel.astype(BF16), expand_ref[...])
        for c in range(t_len // SEL_CHUNK):
            km_ref[c] = km[:, c * SEL_CHUNK:(c + 1) * SEL_CHUNK]

        m_ref[...] = jnp.full((rows3, 1), NEG_BIG, F32)
        l_ref[...] = jnp.zeros((rows3, 1), F32)
        acc_ref[...] = jnp.zeros((rows3, MXU_DIM), F32)

        def sel_body(c, carry):
            k0 = pl.multiple_of(c * SEL_CHUNK, SEL_CHUNK)
            kk = ks_ref[pl.ds(k0, SEL_CHUNK), :].astype(BF16)
            vv = vs_ref[pl.ds(k0, SEL_CHUNK), :].astype(BF16)
            sc = _dot_nt(q3, kk)
            pos = k0 + lax.broadcasted_iota(jnp.int32, (1, SEL_CHUNK), 1)
            kmc = km_ref[c]
            km3 = jnp.concatenate([kmc, kmc, kmc], axis=0)
            valid = (km3 > 0.5) & (pos <= tq3)
            sc = jnp.where(valid, sc - slope3 * (tq3f - pos.astype(F32)), NEG_BIG)
            m_old = m_ref[...]
            m_new = jnp.maximum(m_old, jnp.max(sc, axis=-1, keepdims=True))
            e = jnp.where(valid, jnp.exp(sc - m_new), 0.0)
            alpha = jnp.exp(m_old - m_new)
            l_ref[...] = alpha * l_ref[...] + jnp.sum(e, axis=-1, keepdims=True)
            acc_ref[...] = alpha * acc_ref[...] + _dot(e.astype(BF16), vv)
            m_ref[...] = m_new
            return carry

        lax.fori_loop(0, n_chunks, sel_body, 0)
        o_s = acc_ref[...] / jnp.maximum(l_ref[...], 1e-30)

        n_wc = WINDOW // Q_TILE + 1
        ss, ms, vvs = [], [], []
        for c in range(n_wc):
            kstart = start - WINDOW + Q_TILE * c
            kcl = pl.multiple_of(jnp.maximum(kstart, 0), Q_TILE)
            kk = kw_ref[pl.ds(kcl, Q_TILE), :].astype(BF16)
            vvs.append(vw_ref[pl.ds(kcl, Q_TILE), :].astype(BF16))
            tk = kstart + lax.broadcasted_iota(jnp.int32, (1, Q_TILE), 1)
            dist = tq3 - tk
            ss.append(_dot_nt(q3, kk) - slope3 * dist.astype(F32))
            ms.append((dist >= 0) & (dist < WINDOW) & (tk >= 0))
        p_w = _masked_softmax(jnp.concatenate(ss, axis=-1), jnp.concatenate(ms, axis=-1))
        o_w = jnp.zeros((rows3, MXU_DIM), F32)
        for c in range(n_wc):
            o_w = o_w + _dot(p_w[:, c * Q_TILE:(c + 1) * Q_TILE].astype(BF16), vvs[c])

        in_group = (lane_out >= g * HEAD_DIM) & (lane_out < (g + 1) * HEAD_DIM)
        for r in range(R_A):
            col = 3 * (R_A * g + r)
            rs = slice(r * Q_TILE, (r + 1) * Q_TILE)
            merged = (gate_ref[:, col:col + 1] * o_c[rs] + gate_ref[:, col + 1:col + 2] * o_s[rs]
                      + gate_ref[:, col + 2:col + 3] * o_w[rs])
            outs[r] = jnp.where(in_group, merged, outs[r])

    for r in range(R_A):
        out_ref[:, r * MXU_DIM:(r + 1) * MXU_DIM] = outs[r]


def _nsa_prompt(qa, gates, kc, vc, nsa_rows, win_rows, place, expand, b, t):
    nq = t // Q_TILE
    nsa3 = nsa_rows.reshape(b, t, 4 * D_KV_A)
    win3 = win_rows.reshape(b, t, 2 * D_KV_A)
    kv_spec = lambda col: pl.BlockSpec((None, t, D_KV_A), lambda i, j: (i, 0, col))
    cmp_spec = pl.BlockSpec((None, kc.shape[1], D_KV_A), lambda i, j: (i, 0, 0))
    rows3 = R_A * Q_TILE
    return pl.pallas_call(
        _nsa_prompt_kernel,
        grid=(b, nq),
        in_specs=[pl.BlockSpec((Q_TILE, D_A), lambda i, j: (i * nq + j, 0)),
                  pl.BlockSpec((Q_TILE, LANE), lambda i, j: (i * nq + j, 0)),
                  cmp_spec, cmp_spec, kv_spec(2), kv_spec(3),
                  pl.BlockSpec((None, t, D_KV_A), lambda i, j: (i, 0, 0)),
                  pl.BlockSpec((None, t, D_KV_A), lambda i, j: (i, 0, 1)),
                  _const_spec(place.shape), _const_spec(expand.shape)],
        out_specs=pl.BlockSpec((Q_TILE, D_A), lambda i, j: (i * nq + j, 0)),
        out_shape=jax.ShapeDtypeStruct((b * t, D_A), F32),
        scratch_shapes=[pltpu.VMEM((t // SEL_CHUNK, Q_TILE, SEL_CHUNK), F32),
                        pltpu.VMEM((rows3, 1), F32), pltpu.VMEM((rows3, 1), F32),
                        pltpu.VMEM((rows3, MXU_DIM), F32)],
        compiler_params=_cparams(("arbitrary", "arbitrary")),
        name="nsa_prompt",
    )(qa, gates, kc, vc, nsa3, nsa3, win3, win3, place, expand)


SB_TILE = 256


def _sb_prompt_kernel(q_ref, k_ref, v_ref, u_ref, out_ref, carry_ref, acc_ref):
    qi = pl.program_id(2)
    lane = lax.broadcasted_iota(jnp.int32, (1, LANE), 1)
    q = q_ref[...] * QK_SCALE
    qh = [jnp.where(lane < HEAD_DIM, q, 0.0).astype(BF16), jnp.where(lane >= HEAD_DIM, q, 0.0).astype(BF16)]
    tq = qi * SB_TILE + lax.broadcasted_iota(jnp.int32, (SB_TILE, 1), 0)
    carry_ref[...] = jnp.zeros(carry_ref.shape, F32)
    acc_ref[...] = jnp.zeros(acc_ref.shape, F32)
    u = u_ref[...]

    def body(i, c):
        j = qi - i
        k0 = pl.multiple_of(j * SB_TILE, SB_TILE)
        kk = k_ref[pl.ds(k0, SB_TILE), :].astype(BF16)
        vv = v_ref[pl.ds(k0, SB_TILE), :].astype(BF16)
        tk = k0 + lax.broadcasted_iota(jnp.int32, (1, SB_TILE), 1)
        valid = tk < tq
        for h in range(2):
            z = _dot_nt(qh[h], kk)
            lf_all = _log_sigmoid_neg(z)
            lf = jnp.where(valid, lf_all, 0.0)
            later = carry_ref[h] + _split_dot(lf, u)
            a = jnp.where(valid, jnp.exp(lf_all + z + later), 0.0)
            acc_ref[h] = acc_ref[h] + _dot(a.astype(BF16), vv)
            carry_ref[h] = carry_ref[h] + jnp.sum(lf, axis=-1, keepdims=True)
        return c

    lax.fori_loop(0, qi + 1, body, 0)
    out_ref[...] = jnp.where(lane < HEAD_DIM, acc_ref[0], acc_ref[1])


def _sb_prompt(qb, sb_rows, umat, b, t):
    nq = t // SB_TILE
    n_pair = D_B // LANE
    sb3 = sb_rows.reshape(b, t, 2 * D_B)
    return pl.pallas_call(
        _sb_prompt_kernel,
        grid=(b, n_pair, nq),
        in_specs=[pl.BlockSpec((SB_TILE, LANE), lambda i, h, j: (i * nq + j, h)),
                  pl.BlockSpec((None, t, LANE), lambda i, h, j: (i, 0, h)),
                  pl.BlockSpec((None, t, LANE), lambda i, h, j: (i, 0, n_pair + h)),
                  _const_spec((SB_TILE, SB_TILE))],
        out_specs=pl.BlockSpec((SB_TILE, LANE), lambda i, h, j: (i * nq + j, h)),
        out_shape=jax.ShapeDtypeStruct((b * t, D_B), F32),
        scratch_shapes=[pltpu.VMEM((2, SB_TILE, 1), F32), pltpu.VMEM((2, SB_TILE, LANE), F32)],
        compiler_params=_cparams(("arbitrary", "arbitrary", "arbitrary")),
        name="sb_prompt",
    )(qb, sb3, sb3, umat)


CONV_TILE = 256
CONV_HALO = 32


def _conv_finish(y, cb_ref, lg_ref, lb_ref):
    y = y + cb_ref[...]
    mu = jnp.mean(y, axis=-1, keepdims=True)
    d = y - mu
    var = jnp.mean(d * d, axis=-1, keepdims=True)
    z = d * lax.rsqrt(var + EPS) * lg_ref[...] + lb_ref[...]
    return z * _sigmoid(z)


def _conv_prompt_kernel(cin_ref, halo_ref, cw_ref, cb_ref, lg_ref, lb_ref, out_ref, state_ref, ext_ref):
    ti = pl.program_id(1)
    c = cin_ref[...]
    u = c[:, 0:D_C] * _sigmoid(c[:, D_C:])
    hc = halo_ref[...]
    uh = hc[:, 0:D_C] * _sigmoid(hc[:, D_C:])
    ext_ref[0:CONV_HALO, :] = jnp.where(ti == 0, 0.0, uh)
    ext_ref[CONV_HALO:, :] = u
    off = CONV_HALO - (CONV_W - 1)
    y = jnp.zeros((CONV_TILE, D_C), F32)
    for w in range(CONV_W):
        y = y + ext_ref[off + w:off + w + CONV_TILE, :] * cw_ref[w:w + 1, :]
    out_ref[...] = _conv_finish(y, cb_ref, lg_ref, lb_ref)
    state_ref[...] = ext_ref[CONV_HALO + CONV_TILE - (CONV_W - 1):, :]


def _conv_prompt(cin, cw, cb, lg, lb, b, t):
    nt = t // CONV_TILE
    per = CONV_TILE // CONV_HALO
    return pl.pallas_call(
        _conv_prompt_kernel,
        grid=(b, nt),
        in_specs=[pl.BlockSpec((CONV_TILE, 2 * D_C), lambda i, j: (i * nt + j, 0)),
                  pl.BlockSpec((CONV_HALO, 2 * D_C), lambda i, j: (jnp.maximum((i * nt + j) * per - 1, 0), 0)),
                  _const_spec((CONV_W, D_C)), _const_spec((1, D_C)), _const_spec((1, D_C)), _const_spec((1, D_C))],
        out_specs=[pl.BlockSpec((CONV_TILE, D_C), lambda i, j: (i * nt + j, 0)),
                   pl.BlockSpec((None, CONV_W - 1, D_C), lambda i, j: (i, 0, 0))],
        out_shape=[jax.ShapeDtypeStruct((b * t, D_C), F32), jax.ShapeDtypeStruct((b, CONV_W - 1, D_C), F32)],
        scratch_shapes=[pltpu.VMEM((CONV_HALO + CONV_TILE, D_C), F32)],
        compiler_params=_cparams(("arbitrary", "arbitrary")),
        name="conv_prompt",
    )(cin, cin, cw, cb, lg, lb)


def _conv_sample_kernel(cin_ref, buf_ref, cw_ref, cb_ref, lg_ref, lb_ref, out_ref, state_ref):
    c = cin_ref[...]
    u = c[:, 0:D_C] * _sigmoid(c[:, D_C:])
    buf = buf_ref[...]
    y = jnp.sum(buf * cw_ref[0:CONV_W - 1, :], axis=0, keepdims=True) + u * cw_ref[CONV_W - 1:CONV_W, :]
    out_ref[...] = _conv_finish(y, cb_ref, lg_ref, lb_ref)
    state_ref[0:CONV_W - 2, :] = buf_ref[1:CONV_W - 1, :]
    state_ref[CONV_W - 2:CONV_W - 1, :] = u


def _conv_sample(cin, buf, cw, cb, lg, lb):
    n = cin.shape[0]
    cin3 = cin.reshape(n, 1, 2 * D_C)
    out, state = pl.pallas_call(
        _conv_sample_kernel,
        grid=(n,),
        in_specs=[pl.BlockSpec((None, 1, 2 * D_C), lambda i: (i, 0, 0)),
                  pl.BlockSpec((None, CONV_W - 1, D_C), lambda i: (i, 0, 0)),
                  _const_spec((CONV_W, D_C)), _const_spec((1, D_C)), _const_spec((1, D_C)), _const_spec((1, D_C))],
        out_specs=[pl.BlockSpec((None, 1, D_C), lambda i: (i, 0, 0)),
                   pl.BlockSpec((None, CONV_W - 1, D_C), lambda i: (i, 0, 0))],
        out_shape=[jax.ShapeDtypeStruct((n, 1, D_C), F32), jax.ShapeDtypeStruct((n, CONV_W - 1, D_C), F32)],
        compiler_params=_cparams(("arbitrary",)),
        name="conv_sample",
    )(cin3, buf, cw, cb, lg, lb)
    return out.reshape(n, D_C), state


ROWS_S = R_A * SUBLANE
NSA_PAGES_PER_STEP = 4


def _nsa_sample_kernel(pt_ref, *refs, past_len, page_size):
    npg = NSA_PAGES_PER_STEP
    page_refs = refs[:npg]
    (qa_ref, gate_ref, new_ref, winnew_ref, kc_ref, vc_ref, win_ref,
     out_ref, winout_ref, q_ref, sel_ref, m_ref, l_ref, acc_ref, oc_ref, ow_ref) = refs[npg:]
    j = pl.program_id(1)
    n_steps = pl.num_programs(1)
    tq = past_len
    n_cmp = kc_ref.shape[0]
    n_sel = (past_len + 1 + SEL_BLOCK - 1) // SEL_BLOCK
    sel_w = sel_ref.shape[1]
    rowi = lax.broadcasted_iota(jnp.int32, (ROWS_S, 1), 0)
    g_of_row = rowi & (SUBLANE - 1)
    r_of_row = lax.shift_right_logical(rowi, 3)
    slope = jnp.zeros((ROWS_S, 1), F32)
    for g in range(G_A):
        for r in range(R_A):
            slope = jnp.where((g_of_row == g) & (r_of_row == r), float(SLOPES[g, r]), slope)
    lane256 = lax.broadcasted_iota(jnp.int32, (ROWS_S, MXU_DIM), 1)
    own_lanes = lax.shift_right_logical(lane256, 6) == g_of_row

    @pl.when(j == 0)
    def _():
        qrow = qa_ref[...]
        lane_w = lax.broadcasted_iota(jnp.int32, (ROWS_S, MXU_DIM), 1)
        qmat = jnp.zeros((ROWS_S, MXU_DIM), F32)
        for g in range(G_A):
            for r in range(R_A):
                h = R_A * g + r
                blk = qrow[:, (h // 2) * LANE:(h // 2 + 1) * LANE]
                shift = (g * HEAD_DIM - (h % 2) * HEAD_DIM) % MXU_DIM
                wide = jnp.concatenate([blk, jnp.zeros((1, LANE), F32)], axis=-1)
                placed = pltpu.roll(wide, shift, axis=1) if shift else wide
                keep = (rowi == r * SUBLANE + g) & (lane_w >= g * HEAD_DIM) & (lane_w < (g + 1) * HEAD_DIM)
                qmat = jnp.where(keep, placed, qmat)
        q_ref[...] = (qmat * QK_SCALE).astype(BF16)
        q = q_ref[...]

        s = _dot_nt(q, kc_ref[...].astype(BF16))
        ends = lax.broadcasted_iota(jnp.int32, (1, n_cmp), 1) * CMP_STRIDE + (CMP_LEN - 1)
        s = s - slope * (float(tq) - ends.astype(F32))
        p_c = _masked_softmax(s, ends <= tq)
        oc_ref[...] = _dot(p_c.astype(BF16), vc_ref[...].astype(BF16))
        imp = p_c[0:SUBLANE] + p_c[SUBLANE:2 * SUBLANE] + p_c[2 * SUBLANE:3 * SUBLANE]
        imp = jnp.concatenate([imp, jnp.zeros((SUBLANE, sel_w - n_cmp), F32)], axis=-1)
        tq_col = jnp.full((SUBLANE, 1), tq, jnp.int32)
        sel_ref[...] = _select_blocks(_block_scores(imp, tq_col, n_sel), min(TOP_N, n_sel))

        wb = win_ref.shape[0]
        kw = win_ref[:, 0:D_KV_A].astype(BF16)
        vw = win_ref[:, D_KV_A:].astype(BF16)
        tk = (past_len - wb) + lax.broadcasted_iota(jnp.int32, (1, wb), 1)
        dist = tq - tk
        sw = _dot_nt(q, kw) - slope * dist.astype(F32)
        mw = (dist >= 0) & (dist < WINDOW) & (tk >= 0)
        knew = winnew_ref[:, 0:D_KV_A].astype(BF16).astype(F32)
        vnew = winnew_ref[:, D_KV_A:].astype(BF16).astype(F32)
        s_new = jnp.sum(q.astype(F32) * knew, axis=-1, keepdims=True)
        sw = jnp.where(mw, sw, -jnp.inf)
        mx = jnp.maximum(jnp.max(sw, axis=-1, keepdims=True), s_new)
        e = jnp.where(mw, jnp.exp(sw - mx), 0.0)
        e_new = jnp.exp(s_new - mx)
        den = jnp.maximum(jnp.sum(e, axis=-1, keepdims=True) + e_new, 1e-30)
        p_w = e / den
        p_new = (e_new / den).astype(BF16).astype(F32)
        ow_ref[...] = _dot(p_w.astype(BF16), vw) + p_new * vnew
        rid = lax.broadcasted_iota(jnp.int32, (wb, 1), 0)
        shifted = pltpu.roll(win_ref[...], wb - 1, axis=0)
        winout_ref[...] = jnp.where(rid == wb - 1, winnew_ref[...], shifted)

        m_ref[...] = jnp.full((ROWS_S, 1), NEG_BIG, F32)
        l_ref[...] = jnp.zeros((ROWS_S, 1), F32)
        acc_ref[...] = jnp.zeros((ROWS_S, MXU_DIM), F32)

    q = q_ref[...]
    sel = sel_ref[...].astype(BF16)
    erow = lax.broadcasted_iota(jnp.int32, (sel_w, page_size), 0)
    ecol = lax.broadcasted_iota(jnp.int32, (sel_w, page_size), 1)

    def attend(kk, vv, pos, kmask):
        sc = _dot_nt(q, kk)
        km3 = jnp.concatenate([kmask] * R_A, axis=0)
        valid = (km3 > 0.5) & (pos <= tq)
        sc = jnp.where(valid, sc - slope * (float(tq) - pos.astype(F32)), NEG_BIG)
        m_old = m_ref[...]
        m_new = jnp.maximum(m_old, jnp.max(sc, axis=-1, keepdims=True))
        e = jnp.where(valid, jnp.exp(sc - m_new), 0.0)
        alpha = jnp.exp(m_old - m_new)
        l_ref[...] = alpha * l_ref[...] + jnp.sum(e, axis=-1, keepdims=True)
        acc_ref[...] = alpha * acc_ref[...] + _dot(e.astype(BF16), vv)
        m_ref[...] = m_new

    per_page = page_size // SEL_BLOCK
    for k in range(npg):
        pg = j * npg + k
        blk_of_key = pg * per_page + lax.shift_right_logical(ecol, 6)
        expand = jnp.where(erow == 4 * blk_of_key, 1.0, 0.0).astype(BF16)
        kmask = _dot(sel, expand)
        pos = pg * page_size + lax.broadcasted_iota(jnp.int32, (1, page_size), 1)
        pr = page_refs[k]
        attend(pr[:, 0:D_KV_A].astype(BF16), pr[:, D_KV_A:].astype(BF16), pos, kmask)

    @pl.when(j == n_steps - 1)
    def _():
        knew = new_ref[:, 2 * D_KV_A:3 * D_KV_A].astype(BF16).astype(F32)
        vnew = new_ref[:, 3 * D_KV_A:].astype(BF16).astype(F32)
        blk_new = tq // SEL_BLOCK
        sel_new = sel_ref[:, 4 * blk_new:4 * blk_new + 1]
        ok = jnp.concatenate([sel_new] * R_A, axis=0) > 0.5
        s_new = jnp.where(ok, jnp.sum(q.astype(F32) * knew, axis=-1, keepdims=True), NEG_BIG)
        m_old = m_ref[...]
        m_new = jnp.maximum(m_old, s_new)
        e_new = jnp.where(ok, jnp.exp(s_new - m_new), 0.0)
        alpha = jnp.exp(m_old - m_new)
        l_fin = alpha * l_ref[...] + e_new
        acc = alpha * acc_ref[...] + e_new.astype(BF16).astype(F32) * vnew
        o_s = acc / jnp.maximum(l_fin, 1e-30)

        grow = gate_ref[...]
        lane_g = lax.broadcasted_iota(jnp.int32, (ROWS_S, LANE), 1)
        col0 = 3 * (R_A * g_of_row + r_of_row)
        gsel = lambda c: jnp.sum(jnp.where(lane_g == col0 + c, grow, 0.0), axis=-1, keepdims=True)
        merged = gsel(0) * oc_ref[...] + gsel(1) * o_s + gsel(2) * ow_ref[...]
        merged = jnp.where(own_lanes & (g_of_row < G_A), merged, 0.0)
        for r in range(R_A):
            out_ref[:, r * MXU_DIM:(r + 1) * MXU_DIM] = jnp.sum(
                merged[r * SUBLANE:(r + 1) * SUBLANE], axis=0, keepdims=True)


def _nsa_sample(page_table, cache_l, qa, gates, nsa_new, win_new, kc, vc, state_win_l, past_len):
    n_seq, n_pages = page_table.shape
    page_size = cache_l.shape[1]
    npg = NSA_PAGES_PER_STEP
    n_steps = n_pages // npg
    wb = state_win_l.shape[1]
    n_sel = (past_len + 1 + SEL_BLOCK - 1) // SEL_BLOCK
    sel_w = -(-(4 * n_sel) // LANE) * LANE
    win2 = state_win_l.reshape(n_seq, wb, 2 * D_KV_A)
    tok = lambda w: pl.BlockSpec((None, 1, w), lambda b, j, pt: (b, 0, 0))

    def page_spec(k):
        return pl.BlockSpec((None, page_size, 2 * D_KV_A), lambda b, j, pt: (pt[b, j * npg + k], 0, 1))

    cmp_spec = pl.BlockSpec((None, kc.shape[1], D_KV_A), lambda b, j, pt: (b, 0, 0))
    win_spec = pl.BlockSpec((None, wb, 2 * D_KV_A), lambda b, j, pt: (b, 0, 0))
    grid_spec = pltpu.PrefetchScalarGridSpec(
        num_scalar_prefetch=1,
        grid=(n_seq, n_steps),
        in_specs=[page_spec(k) for k in range(npg)]
        + [tok(D_A), tok(LANE), tok(4 * D_KV_A), tok(2 * D_KV_A), cmp_spec, cmp_spec, win_spec],
        out_specs=[tok(D_A), win_spec],
        scratch_shapes=[pltpu.VMEM((ROWS_S, MXU_DIM), BF16), pltpu.VMEM((SUBLANE, sel_w), F32),
                        pltpu.VMEM((ROWS_S, 1), F32), pltpu.VMEM((ROWS_S, 1), F32),
                        pltpu.VMEM((ROWS_S, MXU_DIM), F32), pltpu.VMEM((ROWS_S, MXU_DIM), F32),
                        pltpu.VMEM((ROWS_S, MXU_DIM), F32)],
    )
    r3 = lambda a: a.reshape(n_seq, 1, a.shape[-1])
    oa, win_out = pl.pallas_call(
        functools.partial(_nsa_sample_kernel, past_len=past_len, page_size=page_size),
        grid_spec=grid_spec,
        out_shape=[jax.ShapeDtypeStruct((n_seq, 1, D_A), F32), jax.ShapeDtypeStruct((n_seq, wb, 2 * D_KV_A), F32)],
        compiler_params=_cparams(("arbitrary", "arbitrary")),
        name="nsa_sample",
    )(page_table, *([cache_l] * npg), r3(qa), r3(gates), r3(nsa_new), r3(win_new), kc, vc, win2)
    return oa.reshape(n_seq, D_A), win_out


ROWS_B = 16
SB_PAGES_PER_STEP = 4


def _sb_sample_kernel(pt_ref, *refs, past_len, page_size, n_pages):
    npg = SB_PAGES_PER_STEP
    page_refs = refs[:npg]
    q_ref, new_ref, u_ref, out_ref, qm_ref, carry_ref, acc_ref = refs[npg:]
    j = pl.program_id(1)
    n_steps = pl.num_programs(1)
    tq = past_len
    rowi = lax.broadcasted_iota(jnp.int32, (ROWS_B, D_B), 0)
    lane = lax.broadcasted_iota(jnp.int32, (ROWS_B, D_B), 1)
    own = lax.shift_right_logical(lane, 6) == rowi

    @pl.when(j == 0)
    def _():
        qm = jnp.where(own, q_ref[...] * QK_SCALE, 0.0)
        qm_ref[...] = qm.astype(BF16)
        acc_ref[...] = jnp.zeros(acc_ref.shape, F32)
        knew = new_ref[:, 0:D_B].astype(BF16).astype(F32)
        z = jnp.sum(qm.astype(BF16).astype(F32) * knew, axis=-1, keepdims=True)
        valid = jnp.full((ROWS_B, 1), past_len < tq)
        lf = jnp.where(valid, _log_sigmoid_neg(z), 0.0)
        a = jnp.where(valid, jnp.exp(_log_sigmoid_neg(z) + z), 0.0)
        acc_ref[...] = a.astype(BF16).astype(F32) * new_ref[:, D_B:].astype(BF16).astype(F32)
        carry_ref[...] = lf

    qm = qm_ref[...]
    u = u_ref[...]
    for k in range(npg):
        pg = (n_pages - 1) - (j * npg + k)
        pr = page_refs[k]
        kk = pr[:, 0:D_B].astype(BF16)
        vv = pr[:, D_B:].astype(BF16)
        tk = pg * page_size + lax.broadcasted_iota(jnp.int32, (1, page_size), 1)
        valid = tk < tq
        z = _dot_nt(qm, kk)
        lf_all = _log_sigmoid_neg(z)
        lf = jnp.where(valid, lf_all, 0.0)
        later = carry_ref[...] + _split_dot(lf, u)
        a = jnp.where(valid, jnp.exp(lf_all + z + later), 0.0)
        acc_ref[...] = acc_ref[...] + _dot(a.astype(BF16), vv)
        carry_ref[...] = carry_ref[...] + jnp.sum(lf, axis=-1, keepdims=True)

    @pl.when(j == n_steps - 1)
    def _():
        out_ref[...] = jnp.sum(jnp.where(own, acc_ref[...], 0.0), axis=0, keepdims=True)


def _sb_sample(page_table, cache_l, qb, sb_new, umat, past_len):
    n_seq, n_pages = page_table.shape
    page_size = cache_l.shape[1]
    npg = SB_PAGES_PER_STEP
    n_steps = n_pages // npg
    tok = lambda w: pl.BlockSpec((None, 1, w), lambda b, j, pt: (b, 0, 0))

    def page_spec(k):
        return pl.BlockSpec((None, page_size, 2 * D_B), lambda b, j, pt: (pt[b, n_pages - 1 - (j * npg + k)], 0, 0))

    grid_spec = pltpu.PrefetchScalarGridSpec(
        num_scalar_prefetch=1,
        grid=(n_seq, n_steps),
        in_specs=[page_spec(k) for k in range(npg)]
        + [tok(D_B), tok(2 * D_B), pl.BlockSpec((page_size, page_size), lambda b, j, pt: (0, 0))],
        out_specs=tok(D_B),
        scratch_shapes=[pltpu.VMEM((ROWS_B, D_B), BF16), pltpu.VMEM((ROWS_B, 1), F32), pltpu.VMEM((ROWS_B, D_B), F32)],
    )
    r3 = lambda a: a.reshape(n_seq, 1, a.shape[-1])
    out = pl.pallas_call(
        functools.partial(_sb_sample_kernel, past_len=past_len, page_size=page_size, n_pages=n_pages),
        grid_spec=grid_spec,
        out_shape=jax.ShapeDtypeStruct((n_seq, 1, D_B), F32),
        compiler_params=_cparams(("arbitrary", "arbitrary")),
        name="sb_sample",
    )(page_table, *([cache_l] * npg), r3(qb), r3(sb_new), umat)
    return out.reshape(n_seq, D_B)


def _oa_perm():
    perm = np.zeros((D_A,), np.int32)
    for r in range(R_A):
        for g in range(G_A):
            for d in range(HEAD_DIM):
                perm[r * MXU_DIM + g * HEAD_DIM + d] = g * R_A * HEAD_DIM + r * HEAD_DIM + d
    return perm


def _place_mats():
    place = np.zeros((H_A, LANE, MXU_DIM), np.float32)
    for g in range(G_A):
        for r in range(R_A):
            h = R_A * g + r
            for d in range(HEAD_DIM):
                place[h, (h % 2) * HEAD_DIM + d, g * HEAD_DIM + d] = 1.0
    return place


def _expand_mat(t):
    e = np.zeros((Q_TILE, t), np.float32)
    for k in range(t):
        e[4 * (k // SEL_BLOCK), k] = 1.0
    return e


def _prep_layer(l, w_in, w_out, norm_mix, norm_ffn, q_norm, k_norm, cmp_pos, cmp_w1, cmp_w2,
                out_norm_a, out_norm_b, conv_w, conv_b, conv_ln_g, conv_ln_b, w_gate, w_up, w_down):
    wi = w_in[l]
    g0 = D_A + 6 * D_KV_A
    w_in_p = jnp.concatenate(
        [wi[:, :g0], wi[:, g0 + N_GATE:], wi[:, g0:g0 + N_GATE], jnp.zeros((D_MODEL, LANE - N_GATE), F32)],
        axis=1).astype(BF16)
    perm = _oa_perm()
    wo = w_out[l]
    w_out_p = jnp.concatenate([wo[:D_A][perm], wo[D_A:]], axis=0).astype(BF16)
    eye = jnp.eye(G_A, dtype=F32)
    w1 = cmp_w1[l].reshape(2, CMP_LEN, HEAD_DIM, HEAD_DIM)
    w1bd = jnp.einsum('gh,kpde->kpgdhe', eye, w1).reshape(2, CMP_LEN, D_KV_A, D_KV_A).astype(BF16)
    w2bd = jnp.einsum('gh,kde->kgdhe', eye, cmp_w2[l]).reshape(2, D_KV_A, D_KV_A).astype(BF16)
    tile4 = lambda v: jnp.tile(v, G_A).reshape(1, D_KV_A)
    return dict(
        w_in_p=w_in_p, w_out_p=w_out_p, w1bd=w1bd, w2bd=w2bd,
        pos4=jnp.tile(cmp_pos[l], (1, 1, G_A)),
        gn_mix=norm_mix[l].reshape(1, D_MODEL), gn_ffn=norm_ffn[l].reshape(1, D_MODEL),
        qn4=tile4(q_norm[l]), kn0_4=tile4(k_norm[l, 0]), kn1_4=tile4(k_norm[l, 1]), kn2_4=tile4(k_norm[l, 2]),
        ga=out_norm_a[l][perm].reshape(1, D_A), gb=out_norm_b[l].reshape(1, D_B),
        cw=conv_w[l], cb=conv_b[l].reshape(1, D_C), lg=conv_ln_g[l].reshape(1, D_C), lb=conv_ln_b[l].reshape(1, D_C),
        wg=w_gate[l].astype(BF16), wu=w_up[l].astype(BF16), wd=w_down[l].astype(BF16),
    )


def kernel(x_prompt, x_sample, cache_nsa, cache_sb, state_win, state_conv, page_table, w_in, w_out, norm_mix,
           norm_ffn, q_norm, k_norm, cmp_pos, cmp_w1, cmp_w2, out_norm_a, out_norm_b, conv_w, conv_b, conv_ln_g,
           conv_ln_b, w_gate, w_up, w_down):
    b, t, _ = x_prompt.shape
    n_seq, dec_seq, _ = x_sample.shape
    assert dec_seq == 1 and t % SB_TILE == 0 and t % (CMP_STRIDE * SUBLANE) == 0 and t >= WINDOW
    n_pool, page_size = cache_nsa.shape[1], cache_nsa.shape[2]
    n_pages = page_table.shape[1]
    past_len = n_pages * page_size
    assert page_size % CMP_STRIDE == 0 and page_size // CMP_STRIDE == SUBLANE
    assert n_pages % CMP_PAGES_PER_STEP == 0 and n_pages % NSA_PAGES_PER_STEP == 0 and n_pages % SB_PAGES_PER_STEP == 0

    pmat = jnp.asarray(np.kron(np.eye(G_A, dtype=np.float32), np.full((HEAD_DIM, HEAD_DIM), 1.0 / HEAD_DIM, np.float32)), BF16)
    place = jnp.asarray(_place_mats(), BF16)
    expand = jnp.asarray(_expand_mat(t), BF16)
    u_prompt = jnp.asarray(np.tril(np.ones((SB_TILE, SB_TILE), np.float32), -1), BF16)
    u_page = jnp.asarray(np.tril(np.ones((page_size, page_size), np.float32), -1), BF16)

    xp = x_prompt.reshape(b * t, D_MODEL)
    xs = x_sample.reshape(n_seq, D_MODEL)
    outs = {k: [] for k in ("p_nsa", "p_sb", "p_win", "p_conv", "s_nsa", "s_sb", "s_win", "s_conv")}
    for l in range(DEPTH):
        w = _prep_layer(l, w_in, w_out, norm_mix, norm_ffn, q_norm, k_norm, cmp_pos, cmp_w1, cmp_w2, out_norm_a,
                        out_norm_b, conv_w, conv_b, conv_ln_g, conv_ln_b, w_gate, w_up, w_down)

        qa, nsa_rows, win_rows, qb, sb_rows, cin, gates = _in_proj(
            xp, w["gn_mix"], w["w_in_p"], w["qn4"], w["kn1_4"], w["kn2_4"], pmat, tm=256)
        kc, vc = _compress_prompt(nsa_rows.reshape(b, t // CMP_STRIDE, CMP_STRIDE * 4 * D_KV_A),
                                  w["pos4"], w["w1bd"], w["w2bd"], w["kn0_4"], pmat)
        oa = _nsa_prompt(qa, gates, kc, vc, nsa_rows, win_rows, place, expand, b, t)
        ob = _sb_prompt(qb, sb_rows, u_prompt, b, t)
        cv, conv_state = _conv_prompt(cin, w["cw"], w["cb"], w["lg"], w["lb"], b, t)
        xp = _out_proj(oa, ob, cv, xp, w["ga"], w["gb"], w["w_out_p"], tm=512)
        xp = _ffn(xp, w["gn_ffn"], w["wg"], w["wu"], w["wd"], tm=1024, tf=512, tn=512)
        outs["p_nsa"].append(nsa_rows.reshape(b, t, 4, G_A, HEAD_DIM))
        outs["p_sb"].append(sb_rows.reshape(b, t, 2, H_B, HEAD_DIM))
        outs["p_win"].append(win_rows.reshape(b, t, 2, G_A, HEAD_DIM)[:, t - min(WINDOW, t):])
        outs["p_conv"].append(conv_state)

        qa, nsa_new, win_new, qb, sb_new, cin, gates = _in_proj(
            xs, w["gn_mix"], w["w_in_p"], w["qn4"], w["kn1_4"], w["kn2_4"], pmat, tm=n_seq)
        cache_l = cache_nsa[l].reshape(n_pool, page_size, 4 * D_KV_A)
        cache16 = cache_nsa[l].reshape(n_pool, page_size // CMP_STRIDE, CMP_STRIDE * 4 * D_KV_A)
        kc, vc = _compress_sample(page_table, cache16, w["pos4"], w["w1bd"], w["w2bd"], w["kn0_4"], pmat)
        oa, win_out = _nsa_sample(page_table, cache_l, qa, gates, nsa_new, win_new, kc, vc, state_win[l], past_len)
        ob = _sb_sample(page_table, cache_sb[l].reshape(n_pool, page_size, 2 * D_B), qb, sb_new, u_page, past_len)
        cv, conv_state = _conv_sample(cin, state_conv[l], w["cw"], w["cb"], w["lg"], w["lb"])
        xs = _out_proj(oa, ob, cv, xs, w["ga"], w["gb"], w["w_out_p"], tm=n_seq)
        xs = _ffn(xs, w["gn_ffn"], w["wg"], w["wu"], w["wd"], tm=n_seq, tf=512, tn=512)
        outs["s_nsa"].append(nsa_new.reshape(n_seq, 1, 4, G_A, HEAD_DIM))
        outs["s_sb"].append(sb_new.reshape(n_seq, 1, 2, H_B, HEAD_DIM))
        outs["s_win"].append(win_out.reshape(n_seq, -1, 2, G_A, HEAD_DIM))
        outs["s_conv"].append(conv_state)

    st = lambda k: jnp.stack(outs[k])
    return (xp.reshape(b, t, D_MODEL), xs.reshape(n_seq, 1, D_MODEL), st("p_nsa"), st("p_sb"), st("p_win"),
            st("p_conv"), st("s_nsa"), st("s_sb"), st("s_win"), st("s_conv"))
```

```python
import functools

import numpy as np
import jax
import jax.numpy as jnp
from jax import lax
from jax.experimental import pallas as pl
from jax.experimental.pallas import tpu as pltpu

F32 = jnp.float32
BF16 = jnp.bfloat16

LANE = 128
SUBLANE = 8
MXU_DIM = 256
VMEM_LIMIT_BYTES = 56 * 1024 * 1024

D_MODEL = 2048
DEPTH = 2
HEAD_DIM = 64
H_A = 12
G_A = 4
R_A = H_A // G_A
H_B = 12
D_A = H_A * HEAD_DIM
D_KV_A = G_A * HEAD_DIM
D_B = H_B * HEAD_DIM
D_C = D_MODEL - D_A - D_B
CONV_W = 31
CMP_LEN = 32
CMP_STRIDE = 16
SEL_BLOCK = 64
TOP_N = 16
WINDOW = 512
FORCE_BONUS = 1000.0
N_GATE = 3 * H_A
D_IN = D_A + 6 * D_KV_A + N_GATE + 3 * D_B + 2 * D_C
D_FF = -(-8 * D_MODEL // (3 * 256)) * 256
EPS = 1e-6
QK_SCALE = HEAD_DIM ** -0.5
NEG_BIG = -1e30

C_QA = 0
C_KV = C_QA + D_A
C_QB = C_KV + 6 * D_KV_A
C_KB = C_QB + D_B
C_VB = C_KB + D_B
C_CIN = C_VB + D_B
C_GATE = C_CIN + 2 * D_C
D_IN_PAD = C_GATE + LANE

SLOPES = (2.0 ** (-8.0 * np.arange(1, H_A + 1, dtype=np.float32) / H_A)).astype(np.float32).reshape(G_A, R_A)


def _cparams(sem):
    return pltpu.CompilerParams(dimension_semantics=sem, vmem_limit_bytes=VMEM_LIMIT_BYTES)


def _const_spec(shape):
    nd = len(shape)
    return pl.BlockSpec(shape, lambda *_: (0,) * nd)


def _dot(a, b):
    return jnp.dot(a, b, preferred_element_type=F32)


def _dot_nt(a, b):
    return lax.dot_general(a, b, (((1,), (1,)), ((), ())), preferred_element_type=F32)


def _split_dot(x, w_bf16):
    hi = x.astype(BF16)
    lo = (x - hi.astype(F32)).astype(BF16)
    return _dot(hi, w_bf16) + _dot(lo, w_bf16)


def _sigmoid(x):
    return 1.0 / (1.0 + jnp.exp(-x))


def _rmsnorm_rows(x, gain):
    ms = jnp.mean(x * x, axis=-1, keepdims=True)
    return x * lax.rsqrt(ms + EPS) * gain


def _head_rmsnorm(u, gain_tile, pmat):
    outs = []
    for c in range(u.shape[1] // MXU_DIM):
        uc = u[:, c * MXU_DIM:(c + 1) * MXU_DIM]
        ms = _split_dot(uc * uc, pmat)
        outs.append(uc * lax.rsqrt(ms + EPS) * gain_tile)
    return outs[0] if len(outs) == 1 else jnp.concatenate(outs, axis=-1)


def _masked_softmax(s, mask):
    s = jnp.where(mask, s, -jnp.inf)
    m = jnp.max(s, axis=-1, keepdims=True)
    m = jnp.where(m == -jnp.inf, 0.0, m)
    e = jnp.where(mask, jnp.exp(s - m), 0.0)
    return e / jnp.maximum(jnp.sum(e, axis=-1, keepdims=True), 1e-30)


def _log_sigmoid_neg(z):
    return -(jnp.maximum(z, 0.0) + jnp.log(1.0 + jnp.exp(-jnp.abs(z))))


def _gelu_tanh(x):
    return 0.5 * x * (1.0 + jnp.tanh(np.float32(np.sqrt(2.0 / np.pi)) * (x + 0.044715 * (x * x * x))))


def _head_rmsnorm_t(u, gain_col, pmat):
    sq = u * u
    hi = sq.astype(BF16)
    lo = (sq - hi.astype(F32)).astype(BF16)
    ms = _dot(pmat, hi) + _dot(pmat, lo)
    return u * lax.rsqrt(ms + EPS) * gain_col


def _in_proj_prompt_kernel(x_ref, gn_ref, wt_ref, qn_ref, kn1c_ref, kn2c_ref, pmat_ref,
                           qa_ref, nsat_ref, wint_ref, qb_ref, sbt_ref, cin_ref, gate_ref, cmp_ref):
    xn = _rmsnorm_rows(x_ref[...], gn_ref[...]).astype(BF16)
    pmat = pmat_ref[...]
    tok = lambda lo, hi: _dot_nt(xn, wt_ref[lo:hi, :])
    feat = lambda lo, hi: _dot_nt(wt_ref[lo:hi, :], xn)

    qa_ref[...] = _head_rmsnorm(tok(C_QA, C_KV), qn_ref[...], pmat)
    kvt = feat(C_KV, C_QB)
    nsat_ref[0:2 * D_KV_A, :] = kvt[0:2 * D_KV_A]
    nsat_ref[2 * D_KV_A:3 * D_KV_A, :] = _head_rmsnorm_t(kvt[2 * D_KV_A:3 * D_KV_A], kn1c_ref[...], pmat)
    nsat_ref[3 * D_KV_A:4 * D_KV_A, :] = kvt[3 * D_KV_A:4 * D_KV_A]
    wint_ref[0:D_KV_A, :] = _head_rmsnorm_t(kvt[4 * D_KV_A:5 * D_KV_A], kn2c_ref[...], pmat)
    wint_ref[D_KV_A:2 * D_KV_A, :] = kvt[5 * D_KV_A:6 * D_KV_A]
    qb_ref[...] = tok(C_QB, C_KB)
    sbt_ref[...] = feat(C_KB, C_CIN)
    cin_ref[...] = tok(C_CIN, C_GATE)
    gate_ref[...] = _sigmoid(tok(C_GATE, D_IN_PAD))
    cmp_ref[...] = tok(C_KV, C_KV + 2 * D_KV_A)


IN_PROJ_TILE = 256


def _in_proj_prompt(x2d, gn, wt, qn4, kn1c, kn2c, pmat, b, t):
    tm = IN_PROJ_TILE
    nt = t // tm
    row = lambda w: pl.BlockSpec((tm, w), lambda i: (i, 0))
    fmaj = lambda f: pl.BlockSpec((None, f, tm), lambda i: (i // nt, 0, i % nt))
    tok_out = lambda w: jax.ShapeDtypeStruct((b * t, w), F32)
    feat_out = lambda f: jax.ShapeDtypeStruct((b, f, t), F32)
    col = _const_spec((MXU_DIM, 1))
    return pl.pallas_call(
        _in_proj_prompt_kernel,
        grid=(b * nt,),
        in_specs=[row(D_MODEL), _const_spec((1, D_MODEL)),
                  pl.BlockSpec((D_IN_PAD, D_MODEL), lambda i: (0, 0), pipeline_mode=pl.Buffered(1)),
                  _const_spec((1, MXU_DIM)), col, col, _const_spec((MXU_DIM, MXU_DIM))],
        out_specs=[row(D_A), fmaj(4 * D_KV_A), fmaj(2 * D_KV_A), row(D_B), fmaj(2 * D_B), row(2 * D_C), row(LANE),
                   row(2 * D_KV_A)],
        out_shape=[tok_out(D_A), feat_out(4 * D_KV_A), feat_out(2 * D_KV_A), tok_out(D_B), feat_out(2 * D_B),
                   tok_out(2 * D_C), tok_out(LANE), tok_out(2 * D_KV_A)],
        compiler_params=_cparams(("arbitrary",)),
        name="in_proj_prompt",
    )(x2d, gn, wt, qn4, kn1c, kn2c, pmat)


def _in_proj_sample_kernel(x_ref, gn_ref, wt_ref, qn_ref, kn1_ref, kn2_ref, kn2c_ref, pmat_ref,
                           qa_ref, nsa_ref, win_ref, qb_ref, sb_ref, cin_ref, gate_ref, wint_ref):
    xn = _rmsnorm_rows(x_ref[...], gn_ref[...]).astype(BF16)
    pmat = pmat_ref[...]
    tok = lambda lo, hi: _dot_nt(xn, wt_ref[lo:hi, :])

    qa_ref[...] = _head_rmsnorm(tok(C_QA, C_KV), qn_ref[...], pmat)
    kv = tok(C_KV, C_QB)
    nsa_ref[:, 0:2 * D_KV_A] = kv[:, 0:2 * D_KV_A]
    nsa_ref[:, 2 * D_KV_A:3 * D_KV_A] = _head_rmsnorm(kv[:, 2 * D_KV_A:3 * D_KV_A], kn1_ref[...], pmat)
    nsa_ref[:, 3 * D_KV_A:4 * D_KV_A] = kv[:, 3 * D_KV_A:4 * D_KV_A]
    win_ref[:, 0:D_KV_A] = _head_rmsnorm(kv[:, 4 * D_KV_A:5 * D_KV_A], kn2_ref[...], pmat)
    win_ref[:, D_KV_A:2 * D_KV_A] = kv[:, 5 * D_KV_A:6 * D_KV_A]
    qb_ref[...] = tok(C_QB, C_KB)
    sb_ref[...] = tok(C_KB, C_CIN)
    cin_ref[...] = tok(C_CIN, C_GATE)
    gate_ref[...] = _sigmoid(tok(C_GATE, D_IN_PAD))
    wt_win = _dot_nt(wt_ref[C_KV + 4 * D_KV_A:C_QB, :], xn)
    wint_ref[0:D_KV_A, :] = _head_rmsnorm_t(wt_win[0:D_KV_A], kn2c_ref[...], pmat)
    wint_ref[D_KV_A:, :] = wt_win[D_KV_A:]


def _in_proj_sample(x2d, gn, wt, qn4, kn1_4, kn2_4, kn2c, pmat):
    m = x2d.shape[0]
    widths = (D_A, 4 * D_KV_A, 2 * D_KV_A, D_B, 2 * D_B, 2 * D_C, LANE)
    full = lambda r, c: _const_spec((r, c))
    return pl.pallas_call(
        _in_proj_sample_kernel,
        grid=(1,),
        in_specs=[full(m, D_MODEL), full(1, D_MODEL),
                  pl.BlockSpec((D_IN_PAD, D_MODEL), lambda i: (0, 0), pipeline_mode=pl.Buffered(1)),
                  full(1, MXU_DIM), full(1, MXU_DIM), full(1, MXU_DIM), full(MXU_DIM, 1), full(MXU_DIM, MXU_DIM)],
        out_specs=[full(m, w) for w in widths] + [full(2 * D_KV_A, m)],
        out_shape=[jax.ShapeDtypeStruct((m, w), F32) for w in widths]
        + [jax.ShapeDtypeStruct((2 * D_KV_A, m), F32)],
        compiler_params=_cparams(("arbitrary",)),
        name="in_proj_sample",
    )(x2d, gn, wt, qn4, kn1_4, kn2_4, kn2c, pmat)


def _out_proj_kernel(oa_ref, ob_ref, cv_ref, x_ref, ga_ref, gb_ref, w_ref, y_ref):
    ha = _rmsnorm_rows(oa_ref[...], ga_ref[...]).astype(BF16)
    hb = _rmsnorm_rows(ob_ref[...], gb_ref[...]).astype(BF16)
    hc = cv_ref[...].astype(BF16)
    acc = _dot(ha, w_ref[0:D_A, :]) + _dot(hb, w_ref[D_A:D_A + D_B, :]) + _dot(hc, w_ref[D_A + D_B:, :])
    y_ref[...] = x_ref[...] + acc


def _out_proj(oa, ob, cv, x2d, ga, gb, w_out_p, tm):
    m = x2d.shape[0]
    row = lambda w: pl.BlockSpec((tm, w), lambda i: (i, 0))
    return pl.pallas_call(
        _out_proj_kernel,
        grid=(m // tm,),
        in_specs=[row(D_A), row(D_B), row(D_C), row(D_MODEL), _const_spec((1, D_A)), _const_spec((1, D_B)),
                  _const_spec((D_MODEL, D_MODEL))],
        out_specs=row(D_MODEL),
        out_shape=jax.ShapeDtypeStruct((m, D_MODEL), F32),
        compiler_params=_cparams(("arbitrary",)),
        name="out_proj",
    )(oa, ob, cv, x2d, ga, gb, w_out_p)


def _ffn_up_kernel(x_ref, gn_ref, wg_ref, wu_ref, h_ref, xn_ref):
    @pl.when(pl.program_id(1) == 0)
    def _():
        xn_ref[...] = _rmsnorm_rows(x_ref[...], gn_ref[...]).astype(BF16)

    xn = xn_ref[...]
    a = _dot(xn, wg_ref[...])
    b = _dot(xn, wu_ref[...])
    h_ref[...] = (a * _sigmoid(a) * b).astype(BF16)


def _ffn_down_kernel(h_ref, x_ref, wd_ref, y_ref):
    y_ref[...] = x_ref[...] + _dot(h_ref[...], wd_ref[...])


def _ffn(x2d, gn, wg, wu, wd, tm, tf, tn):
    m = x2d.shape[0]
    h = pl.pallas_call(
        _ffn_up_kernel,
        grid=(m // tm, D_FF // tf),
        in_specs=[pl.BlockSpec((tm, D_MODEL), lambda i, j: (i, 0)), _const_spec((1, D_MODEL)),
                  pl.BlockSpec((D_MODEL, tf), lambda i, j: (0, j)), pl.BlockSpec((D_MODEL, tf), lambda i, j: (0, j))],
        out_specs=pl.BlockSpec((tm, tf), lambda i, j: (i, j)),
        out_shape=jax.ShapeDtypeStruct((m, D_FF), BF16),
        scratch_shapes=[pltpu.VMEM((tm, D_MODEL), BF16)],
        compiler_params=_cparams(("arbitrary", "arbitrary")),
        name="ffn_up",
    )(x2d, gn, wg, wu)
    return pl.pallas_call(
        _ffn_down_kernel,
        grid=(m // tm, D_MODEL // tn),
        in_specs=[pl.BlockSpec((tm, D_FF), lambda i, j: (i, 0)), pl.BlockSpec((tm, tn), lambda i, j: (i, j)),
                  pl.BlockSpec((D_FF, tn), lambda i, j: (0, j))],
        out_specs=pl.BlockSpec((tm, tn), lambda i, j: (i, j)),
        out_shape=jax.ShapeDtypeStruct((m, D_MODEL), F32),
        compiler_params=_cparams(("arbitrary", "arbitrary")),
        name="ffn_down",
    )(h, x2d, wd)


def _compress_rows(get_piece, n_rows, pos_ref, w1_ref, w2_ref, kn0_ref, pmat_ref):
    outs = []
    for kv in range(2):
        first = jnp.zeros((n_rows, MXU_DIM), F32)
        second = jnp.zeros((n_rows, MXU_DIM), F32)
        for p in range(CMP_STRIDE):
            xp = get_piece(p, kv)
            first = first + _dot((xp + pos_ref[kv, p:p + 1, :]).astype(BF16), w1_ref[kv, p])
            q = CMP_STRIDE + p
            second = second + _dot((xp + pos_ref[kv, q:q + 1, :]).astype(BF16), w1_ref[kv, q])
        pre = first + pltpu.roll(second, n_rows - 1, axis=0)
        o = _dot(_gelu_tanh(pre).astype(BF16), w2_ref[kv])
        if kv == 0:
            o = _head_rmsnorm(o, kn0_ref[...], pmat_ref[...])
        outs.append(o)
    return outs


def _compress_prompt_kernel(x_ref, pos_ref, w1_ref, w2_ref, kn0_ref, pmat_ref, kc_ref, vc_ref):
    n_rows = x_ref.shape[0]
    row_w = 2 * D_KV_A

    def piece(p, kv):
        lo = p * row_w + kv * D_KV_A
        return x_ref[:, lo:lo + D_KV_A]

    kc, vc = _compress_rows(piece, n_rows, pos_ref, w1_ref, w2_ref, kn0_ref, pmat_ref)
    kc_ref[...] = kc
    vc_ref[...] = vc


def _compress_prompt(nsa16, pos4, w1bd, w2bd, kn0_4, pmat):
    b, nsub, w = nsa16.shape
    return pl.pallas_call(
        _compress_prompt_kernel,
        grid=(b,),
        in_specs=[pl.BlockSpec((None, nsub, w), lambda i: (i, 0, 0)),
                  _const_spec(pos4.shape), _const_spec(w1bd.shape), _const_spec(w2bd.shape),
                  _const_spec((1, MXU_DIM)), _const_spec((MXU_DIM, MXU_DIM))],
        out_specs=[pl.BlockSpec((None, nsub, D_KV_A), lambda i: (i, 0, 0))] * 2,
        out_shape=[jax.ShapeDtypeStruct((b, nsub, D_KV_A), F32)] * 2,
        compiler_params=_cparams(("arbitrary",)),
        name="compress_prompt",
    )(nsa16, pos4, w1bd, w2bd, kn0_4, pmat)


CMP_PAGES_PER_STEP = 32


def _compress_sample_kernel(pt_ref, *refs, page_size):
    npg = CMP_PAGES_PER_STEP
    page_refs = refs[:npg + 1]
    pos_ref, w1_ref, w2_ref, kn0_ref, pmat_ref, kc_ref, vc_ref, rows_ref = refs[npg + 1:]
    sub = page_size // CMP_STRIDE
    n_rows = (npg + 1) * sub
    n_slab = 2 * D_KV_A // LANE

    for k, pr in enumerate(page_refs):
        for s in range(n_slab):
            rows_ref[s, k * page_size:(k + 1) * page_size, :] = pr[s * LANE:(s + 1) * LANE, :].T

    def piece(p, kv):
        slabs = [rows_ref[2 * kv + h, pl.ds(p, n_rows, stride=CMP_STRIDE), :] for h in range(2)]
        return jnp.concatenate(slabs, axis=-1)

    kc, vc = _compress_rows(piece, n_rows, pos_ref, w1_ref, w2_ref, kn0_ref, pmat_ref)
    kc_ref[...] = kc[0:npg * sub]
    vc_ref[...] = vc[0:npg * sub]


def _compress_sample(page_table, cache_t, layer, pos4, w1bd, w2bd, kn0_4, pmat):
    n_seq, n_pages = page_table.shape
    page_size = cache_t.shape[-1]
    npg = CMP_PAGES_PER_STEP
    n_steps = n_pages // npg
    sub = page_size // CMP_STRIDE

    def page_spec(k):
        def imap(b, j, pt):
            return (layer, pt[b, jnp.minimum(j * npg + k, n_pages - 1)], 0, 0)
        return pl.BlockSpec((None, None, 2 * D_KV_A, page_size), imap)

    const = lambda shape: pl.BlockSpec(shape, lambda b, j, pt: (0,) * len(shape))
    out_spec = pl.BlockSpec((None, npg * sub, D_KV_A), lambda b, j, pt: (b, j, 0))
    grid_spec = pltpu.PrefetchScalarGridSpec(
        num_scalar_prefetch=1,
        grid=(n_seq, n_steps),
        in_specs=[page_spec(k) for k in range(npg + 1)]
        + [const(pos4.shape), const(w1bd.shape), const(w2bd.shape), const((1, MXU_DIM)), const((MXU_DIM, MXU_DIM))],
        out_specs=[out_spec, out_spec],
        scratch_shapes=[pltpu.VMEM((2 * D_KV_A // LANE, (npg + 1) * page_size, LANE), F32)],
    )
    return pl.pallas_call(
        functools.partial(_compress_sample_kernel, page_size=page_size),
        grid_spec=grid_spec,
        out_shape=[jax.ShapeDtypeStruct((n_seq, n_pages * sub, D_KV_A), F32)] * 2,
        compiler_params=_cparams(("arbitrary", "arbitrary")),
        name="compress_sample",
    )(page_table, *([cache_t] * (npg + 1)), pos4, w1bd, w2bd, kn0_4, pmat)


def _select_blocks(score, n_top):
    width = score.shape[1]
    per = SEL_BLOCK // CMP_STRIDE
    lane = lax.broadcasted_iota(jnp.int32, (1, width), 1)
    rank = jnp.zeros(score.shape, F32)
    for k in range(1, width // per):
        other = pltpu.roll(score, per * k, axis=1)
        from_lower = jnp.where(lane >= per * k, 1.0, 0.0)
        rank = rank + jnp.where(other > score, 1.0, jnp.where(other == score, from_lower, 0.0))
    return jnp.where(rank < n_top, jnp.where(score >= 0.0, 1.0, 0.0), 0.0)


def _block_scores(imp, tq_col, n_sel):
    width = imp.shape[1]
    per = SEL_BLOCK // CMP_STRIDE
    pooled = imp
    for k in range(1, per):
        pooled = pooled + pltpu.roll(imp, width - k, axis=1)
    lane = lax.broadcasted_iota(jnp.int32, imp.shape, 1)
    blk = lax.shift_right_logical(lane, 2)
    cur = lax.shift_right_logical(tq_col, 6)
    valid = blk <= cur
    forced = (blk == 0) | (blk == cur) | (blk == cur - 1)
    score = jnp.where(valid, jnp.where(forced, FORCE_BONUS, pooled), -1.0)
    cand = ((lane & (per - 1)) == 0) & (blk < n_sel)
    return jnp.where(cand, score, -jnp.inf)


Q_TILE = 128
SEL_CHUNK = 512
N_WIN_BLOCKS = WINDOW // Q_TILE + 1


def _nsa_prompt_kernel(qa_ref, gate_ref, kc_ref, vc_ref, ksv_ref, w0_ref, w1_ref, w2_ref, w3_ref, w4_ref,
                       place_ref, expand_ref, out_ref, km_ref, s_ref, mx_ref, l_ref, acc_ref):
    win_refs = (w0_ref, w1_ref, w2_ref, w3_ref, w4_ref)
    t_len = ksv_ref.shape[1]
    n_cmp = kc_ref.shape[0]
    n_sel = t_len // SEL_BLOCK
    n_chunk_total = t_len // SEL_CHUNK
    qi = pl.program_id(1)
    start = qi * Q_TILE
    rows3 = R_A * Q_TILE
    rowi = lax.broadcasted_iota(jnp.int32, (rows3, 1), 0)
    r_of_row = lax.shift_right_logical(rowi, 7)
    tq3 = start + (rowi & (Q_TILE - 1))
    tq3f = tq3.astype(F32)
    tq1 = tq3[0:Q_TILE]
    lane_out = lax.broadcasted_iota(jnp.int32, (Q_TILE, MXU_DIM), 1)
    n_chunks = lax.shift_right_logical(start + Q_TILE + SEL_CHUNK - 1, 9)

    q3s, slopes, o_cs, scores = [], [], [], []
    kc = kc_ref[...].astype(BF16)
    vc = vc_ref[...].astype(BF16)
    ends = lax.broadcasted_iota(jnp.int32, (1, n_cmp), 1) * CMP_STRIDE + (CMP_LEN - 1)
    for g in range(G_A):
        qs = []
        for r in range(R_A):
            h = R_A * g + r
            qblk = qa_ref[:, (h // 2) * LANE:(h // 2 + 1) * LANE].astype(BF16)
            qs.append(_dot(qblk, place_ref[h]) * QK_SCALE)
        q3 = jnp.concatenate(qs, axis=0).astype(BF16)
        slope3 = jnp.where(r_of_row == 0, float(SLOPES[g, 0]),
                           jnp.where(r_of_row == 1, float(SLOPES[g, 1]), float(SLOPES[g, 2])))
        s = _dot_nt(q3, kc) - slope3 * (tq3f - ends.astype(F32))
        p_c = _masked_softmax(s, ends <= tq3)
        o_cs.append(_dot(p_c.astype(BF16), vc))
        imp = p_c[0:Q_TILE] + p_c[Q_TILE:2 * Q_TILE] + p_c[2 * Q_TILE:3 * Q_TILE]
        scores.append(_block_scores(imp, tq1, n_sel))
        q3s.append(q3)
        slopes.append(slope3)

    sel = _select_blocks(jnp.concatenate(scores, axis=0), min(TOP_N, n_sel))
    km_ref[...] = _dot(sel.astype(BF16), expand_ref[...])

    mx_ref[...] = jnp.full(mx_ref.shape, NEG_BIG, F32)
    l_ref[...] = jnp.zeros(l_ref.shape, F32)
    acc_ref[...] = jnp.zeros(acc_ref.shape, F32)
    for c in range(n_chunk_total):
        lo, hi = c * SEL_CHUNK, (c + 1) * SEL_CHUNK

        @pl.when(c < n_chunks)
        def _(lo=lo, hi=hi):
            kk = ksv_ref[0:D_KV_A, lo:hi].astype(BF16)
            pos = lo + lax.broadcasted_iota(jnp.int32, (1, SEL_CHUNK), 1)
            dist = tq3f - pos.astype(F32)
            causal = pos <= tq3
            for g in range(G_A):
                kmc = km_ref[g * Q_TILE:(g + 1) * Q_TILE, lo:hi]
                valid = (jnp.concatenate([kmc, kmc, kmc], axis=0) > 0.5) & causal
                sc = jnp.where(valid, _dot(q3s[g], kk) - slopes[g] * dist, NEG_BIG)
                s_ref[g, :, lo:hi] = sc
                m = mx_ref[g]
                for k in range(SEL_CHUNK // LANE):
                    m = jnp.maximum(m, sc[:, k * LANE:(k + 1) * LANE])
                mx_ref[g] = m

    row_max = [jnp.max(mx_ref[g], axis=-1, keepdims=True) for g in range(G_A)]
    for c in range(n_chunk_total):
        lo, hi = c * SEL_CHUNK, (c + 1) * SEL_CHUNK

        @pl.when(c < n_chunks)
        def _(lo=lo, hi=hi):
            vv = ksv_ref[D_KV_A:2 * D_KV_A, lo:hi].astype(BF16)
            for g in range(G_A):
                sc = s_ref[g, :, lo:hi]
                e = jnp.where(sc > 0.5 * NEG_BIG, jnp.exp(sc - row_max[g]), 0.0)
                part = l_ref[g]
                for k in range(SEL_CHUNK // LANE):
                    part = part + e[:, k * LANE:(k + 1) * LANE]
                l_ref[g] = part
                acc_ref[g] = acc_ref[g] + _dot_nt(e.astype(BF16), vv)

    outs = [jnp.zeros((Q_TILE, MXU_DIM), F32) for _ in range(R_A)]
    for g in range(G_A):
        q3, slope3 = q3s[g], slopes[g]
        o_s = acc_ref[g] / jnp.maximum(jnp.sum(l_ref[g], axis=-1, keepdims=True), 1e-30)

        ss, ms = [], []
        for c in range(N_WIN_BLOCKS):
            tk = start - WINDOW + Q_TILE * c + lax.broadcasted_iota(jnp.int32, (1, Q_TILE), 1)
            dist = tq3 - tk
            ss.append(_dot(q3, win_refs[c][0:D_KV_A, :].astype(BF16)) - slope3 * dist.astype(F32))
            ms.append((dist >= 0) & (dist < WINDOW) & (tk >= 0))
        p_w = _masked_softmax(jnp.concatenate(ss, axis=-1), jnp.concatenate(ms, axis=-1))
        o_w = jnp.zeros((rows3, MXU_DIM), F32)
        for c in range(N_WIN_BLOCKS):
            o_w = o_w + _dot_nt(p_w[:, c * Q_TILE:(c + 1) * Q_TILE].astype(BF16),
                                win_refs[c][D_KV_A:2 * D_KV_A, :].astype(BF16))

        in_group = (lane_out >= g * HEAD_DIM) & (lane_out < (g + 1) * HEAD_DIM)
        for r in range(R_A):
            col = 3 * (R_A * g + r)
            rs = slice(r * Q_TILE, (r + 1) * Q_TILE)
            merged = (gate_ref[:, col:col + 1] * o_cs[g][rs] + gate_ref[:, col + 1:col + 2] * o_s[rs]
                      + gate_ref[:, col + 2:col + 3] * o_w[rs])
            outs[r] = jnp.where(in_group, merged, outs[r])

    for r in range(R_A):
        out_ref[:, r * MXU_DIM:(r + 1) * MXU_DIM] = outs[r]


def _nsa_prompt(qa, gates, kc, vc, nsa_t, win_t, place, expand, b, t):
    nq = t // Q_TILE
    cmp_spec = pl.BlockSpec((None, kc.shape[1], D_KV_A), lambda i, j: (i, 0, 0))
    rows3 = R_A * Q_TILE

    def win_spec(c):
        return pl.BlockSpec((None, 2 * D_KV_A, Q_TILE),
                            lambda i, j: (i, 0, jnp.maximum(j - (N_WIN_BLOCKS - 1) + c, 0)))

    return pl.pallas_call(
        _nsa_prompt_kernel,
        grid=(b, nq),
        in_specs=[pl.BlockSpec((Q_TILE, D_A), lambda i, j: (i * nq + j, 0)),
                  pl.BlockSpec((Q_TILE, LANE), lambda i, j: (i * nq + j, 0)),
                  cmp_spec, cmp_spec,
                  pl.BlockSpec((None, 2 * D_KV_A, t), lambda i, j: (i, 1, 0))]
        + [win_spec(c) for c in range(N_WIN_BLOCKS)]
        + [_const_spec(place.shape), _const_spec(expand.shape)],
        out_specs=pl.BlockSpec((Q_TILE, D_A), lambda i, j: (i * nq + j, 0)),
        out_shape=jax.ShapeDtypeStruct((b * t, D_A), F32),
        scratch_shapes=[pltpu.VMEM((G_A * Q_TILE, t), F32),
                        pltpu.VMEM((G_A, rows3, t), F32),
                        pltpu.VMEM((G_A, rows3, LANE), F32), pltpu.VMEM((G_A, rows3, LANE), F32),
                        pltpu.VMEM((G_A, rows3, MXU_DIM), F32)],
        compiler_params=_cparams(("arbitrary", "arbitrary")),
        name="nsa_prompt",
    )(qa, gates, kc, vc, nsa_t, *([win_t] * N_WIN_BLOCKS), place, expand)


SB_TILE = 256


def _sb_prompt_kernel(q_ref, kt_ref, vt_ref, u_ref, out_ref, carry_ref, acc_ref):
    qi = pl.program_id(2)
    n_chunk_total = kt_ref.shape[1] // SB_TILE
    lane = lax.broadcasted_iota(jnp.int32, (1, LANE), 1)
    q = q_ref[...] * QK_SCALE
    qh = [jnp.where(lane < HEAD_DIM, q, 0.0).astype(BF16), jnp.where(lane >= HEAD_DIM, q, 0.0).astype(BF16)]
    tq = qi * SB_TILE + lax.broadcasted_iota(jnp.int32, (SB_TILE, 1), 0)
    carry_ref[...] = jnp.zeros(carry_ref.shape, F32)
    acc_ref[...] = jnp.zeros(acc_ref.shape, F32)

    for j in reversed(range(n_chunk_total)):
        lo, hi = j * SB_TILE, (j + 1) * SB_TILE

        @pl.when(j <= qi)
        def _(lo=lo, hi=hi):
            kk = kt_ref[:, lo:hi].astype(BF16)
            vv = vt_ref[:, lo:hi].astype(BF16)
            tk = lo + lax.broadcasted_iota(jnp.int32, (1, SB_TILE), 1)
            valid = tk < tq
            u = u_ref[...]
            for h in range(2):
                z = _dot(qh[h], kk)
                lf_all = _log_sigmoid_neg(z)
                lf = jnp.where(valid, lf_all, 0.0)
                later = carry_ref[h] + _split_dot(lf, u)
                a = jnp.where(valid, jnp.exp(lf_all + z + later), 0.0)
                acc_ref[h] = acc_ref[h] + _dot_nt(a.astype(BF16), vv)
                carry_ref[h] = carry_ref[h] + jnp.sum(lf, axis=-1, keepdims=True)

    out_ref[...] = jnp.where(lane < HEAD_DIM, acc_ref[0], acc_ref[1])


def _sb_prompt(qb, sb_t, umat, b, t):
    nq = t // SB_TILE
    n_pair = D_B // LANE
    return pl.pallas_call(
        _sb_prompt_kernel,
        grid=(b, n_pair, nq),
        in_specs=[pl.BlockSpec((SB_TILE, LANE), lambda i, h, j: (i * nq + j, h)),
                  pl.BlockSpec((None, LANE, t), lambda i, h, j: (i, h, 0)),
                  pl.BlockSpec((None, LANE, t), lambda i, h, j: (i, n_pair + h, 0)),
                  _const_spec((SB_TILE, SB_TILE))],
        out_specs=pl.BlockSpec((SB_TILE, LANE), lambda i, h, j: (i * nq + j, h)),
        out_shape=jax.ShapeDtypeStruct((b * t, D_B), F32),
        scratch_shapes=[pltpu.VMEM((2, SB_TILE, 1), F32), pltpu.VMEM((2, SB_TILE, LANE), F32)],
        compiler_params=_cparams(("arbitrary", "arbitrary", "arbitrary")),
        name="sb_prompt",
    )(qb, sb_t, sb_t, umat)


CONV_TILE = 256
CONV_HALO = 32


def _conv_finish(y, cb_ref, lg_ref, lb_ref):
    y = y + cb_ref[...]
    mu = jnp.mean(y, axis=-1, keepdims=True)
    d = y - mu
    var = jnp.mean(d * d, axis=-1, keepdims=True)
    z = d * lax.rsqrt(var + EPS) * lg_ref[...] + lb_ref[...]
    return z * _sigmoid(z)


def _conv_prompt_kernel(cin_ref, halo_ref, cw_ref, cb_ref, lg_ref, lb_ref, out_ref, state_ref, ext_ref):
    ti = pl.program_id(1)
    c = cin_ref[...]
    u = c[:, 0:D_C] * _sigmoid(c[:, D_C:])
    hc = halo_ref[...]
    uh = hc[:, 0:D_C] * _sigmoid(hc[:, D_C:])
    ext_ref[0:CONV_HALO, :] = jnp.where(ti == 0, 0.0, uh)
    ext_ref[CONV_HALO:, :] = u
    off = CONV_HALO - (CONV_W - 1)
    y = jnp.zeros((CONV_TILE, D_C), F32)
    for w in range(CONV_W):
        y = y + ext_ref[off + w:off + w + CONV_TILE, :] * cw_ref[w:w + 1, :]
    out_ref[...] = _conv_finish(y, cb_ref, lg_ref, lb_ref)
    state_ref[...] = ext_ref[CONV_HALO + CONV_TILE - (CONV_W - 1):, :]


def _conv_prompt(cin, cw, cb, lg, lb, b, t):
    nt = t // CONV_TILE
    per = CONV_TILE // CONV_HALO
    return pl.pallas_call(
        _conv_prompt_kernel,
        grid=(b, nt),
        in_specs=[pl.BlockSpec((CONV_TILE, 2 * D_C), lambda i, j: (i * nt + j, 0)),
                  pl.BlockSpec((CONV_HALO, 2 * D_C), lambda i, j: (jnp.maximum((i * nt + j) * per - 1, 0), 0)),
                  _const_spec((CONV_W, D_C)), _const_spec((1, D_C)), _const_spec((1, D_C)), _const_spec((1, D_C))],
        out_specs=[pl.BlockSpec((CONV_TILE, D_C), lambda i, j: (i * nt + j, 0)),
                   pl.BlockSpec((None, CONV_W - 1, D_C), lambda i, j: (i, 0, 0))],
        out_shape=[jax.ShapeDtypeStruct((b * t, D_C), F32), jax.ShapeDtypeStruct((b, CONV_W - 1, D_C), F32)],
        scratch_shapes=[pltpu.VMEM((CONV_HALO + CONV_TILE, D_C), F32)],
        compiler_params=_cparams(("arbitrary", "arbitrary")),
        name="conv_prompt",
    )(cin, cin, cw, cb, lg, lb)


def _conv_sample_kernel(cin_ref, buf_ref, cw_ref, cb_ref, lg_ref, lb_ref, out_ref, state_ref):
    c = cin_ref[...]
    u = c[:, 0:D_C] * _sigmoid(c[:, D_C:])
    y = u * cw_ref[CONV_W - 1:CONV_W, :]
    for w in range(CONV_W - 1):
        y = y + buf_ref[w] * cw_ref[w:w + 1, :]
    out_ref[...] = _conv_finish(y, cb_ref, lg_ref, lb_ref)
    for w in range(CONV_W - 2):
        state_ref[w] = buf_ref[w + 1]
    state_ref[CONV_W - 2] = u


def _conv_sample(cin, buf_t, layer, cw, cb, lg, lb):
    n = cin.shape[0]
    state_spec = pl.BlockSpec((CONV_W - 1, n, D_C), lambda i: (0, 0, 0))
    return pl.pallas_call(
        _conv_sample_kernel,
        grid=(1,),
        in_specs=[_const_spec((n, 2 * D_C)),
                  pl.BlockSpec((None, CONV_W - 1, n, D_C), lambda i: (layer, 0, 0, 0)),
                  _const_spec((CONV_W, D_C)), _const_spec((1, D_C)), _const_spec((1, D_C)), _const_spec((1, D_C))],
        out_specs=[_const_spec((n, D_C)), state_spec],
        out_shape=[jax.ShapeDtypeStruct((n, D_C), F32), jax.ShapeDtypeStruct((CONV_W - 1, n, D_C), F32)],
        compiler_params=_cparams(("arbitrary",)),
        name="conv_sample",
    )(cin, buf_t, cw, cb, lg, lb)


ROWS_S = R_A * SUBLANE
NSA_PAGES_PER_STEP = 8


def _nsa_sample_kernel(pt_ref, *refs, past_len, page_size):
    npg = NSA_PAGES_PER_STEP
    page_refs = refs[:npg]
    (qa_ref, gate_ref, new_ref, winnew_ref, winnewt_ref, kc_ref, vc_ref, win_ref,
     out_ref, winout_ref, q_ref, sel_ref, m_ref, l_ref, acc_ref, oc_ref, ow_ref) = refs[npg:]
    b = pl.program_id(0)
    j = pl.program_id(1)
    n_steps = pl.num_programs(1)
    tq = past_len
    n_cmp = kc_ref.shape[0]
    n_sel = (past_len + 1 + SEL_BLOCK - 1) // SEL_BLOCK
    sel_w = sel_ref.shape[1]
    rowi = lax.broadcasted_iota(jnp.int32, (ROWS_S, 1), 0)
    g_of_row = rowi & (SUBLANE - 1)
    r_of_row = lax.shift_right_logical(rowi, 3)
    slope = jnp.zeros((ROWS_S, 1), F32)
    for g in range(G_A):
        for r in range(R_A):
            slope = jnp.where((g_of_row == g) & (r_of_row == r), float(SLOPES[g, r]), slope)
    lane256 = lax.broadcasted_iota(jnp.int32, (ROWS_S, MXU_DIM), 1)
    own_lanes = lax.shift_right_logical(lane256, 6) == g_of_row

    @pl.when(j == 0)
    def _():
        qrow = qa_ref[...]
        lane_w = lax.broadcasted_iota(jnp.int32, (ROWS_S, MXU_DIM), 1)
        qmat = jnp.zeros((ROWS_S, MXU_DIM), F32)
        for g in range(G_A):
            for r in range(R_A):
                h = R_A * g + r
                blk = qrow[:, (h // 2) * LANE:(h // 2 + 1) * LANE]
                shift = (g * HEAD_DIM - (h % 2) * HEAD_DIM) % MXU_DIM
                wide = jnp.concatenate([blk, jnp.zeros((1, LANE), F32)], axis=-1)
                placed = pltpu.roll(wide, shift, axis=1) if shift else wide
                keep = (rowi == r * SUBLANE + g) & (lane_w >= g * HEAD_DIM) & (lane_w < (g + 1) * HEAD_DIM)
                qmat = jnp.where(keep, placed, qmat)
        q_ref[...] = (qmat * QK_SCALE).astype(BF16)
        q = q_ref[...]

        s = _dot_nt(q, kc_ref[...].astype(BF16))
        ends = lax.broadcasted_iota(jnp.int32, (1, n_cmp), 1) * CMP_STRIDE + (CMP_LEN - 1)
        s = s - slope * (float(tq) - ends.astype(F32))
        p_c = _masked_softmax(s, ends <= tq)
        oc_ref[...] = _dot(p_c.astype(BF16), vc_ref[...].astype(BF16))
        imp = p_c[0:SUBLANE] + p_c[SUBLANE:2 * SUBLANE] + p_c[2 * SUBLANE:3 * SUBLANE]
        imp = jnp.concatenate([imp, jnp.zeros((SUBLANE, sel_w - n_cmp), F32)], axis=-1)
        tq_col = jnp.full((SUBLANE, 1), tq, jnp.int32)
        sel_ref[...] = _select_blocks(_block_scores(imp, tq_col, n_sel), min(TOP_N, n_sel))

        wb = win_ref.shape[1]
        kw = win_ref[0:D_KV_A, :].astype(BF16)
        vw = win_ref[D_KV_A:, :].astype(BF16)
        tk = (past_len - wb) + lax.broadcasted_iota(jnp.int32, (1, wb), 1)
        dist = tq - tk
        sw = _dot(q, kw) - slope * dist.astype(F32)
        mw = (dist >= 0) & (dist < WINDOW) & (tk >= 0)
        knew = winnew_ref[:, 0:D_KV_A].astype(BF16).astype(F32)
        vnew = winnew_ref[:, D_KV_A:].astype(BF16).astype(F32)
        s_new = jnp.sum(q.astype(F32) * knew, axis=-1, keepdims=True)
        sw = jnp.where(mw, sw, -jnp.inf)
        mx = jnp.maximum(jnp.max(sw, axis=-1, keepdims=True), s_new)
        e = jnp.where(mw, jnp.exp(sw - mx), 0.0)
        e_new = jnp.exp(s_new - mx)
        den = jnp.maximum(jnp.sum(e, axis=-1, keepdims=True) + e_new, 1e-30)
        p_w = e / den
        p_new = (e_new / den).astype(BF16).astype(F32)
        ow_ref[...] = _dot_nt(p_w.astype(BF16), vw) + p_new * vnew
        seq_lane = lax.broadcasted_iota(jnp.int32, winnewt_ref.shape, 1)
        new_col = jnp.sum(jnp.where(seq_lane == b, winnewt_ref[...], 0.0), axis=-1, keepdims=True)
        pos_lane = lax.broadcasted_iota(jnp.int32, (1, wb), 1)
        shifted = pltpu.roll(win_ref[...], wb - 1, axis=1)
        winout_ref[...] = jnp.where(pos_lane == wb - 1, new_col, shifted)

        m_ref[...] = jnp.full((ROWS_S, 1), NEG_BIG, F32)
        l_ref[...] = jnp.zeros((ROWS_S, 1), F32)
        acc_ref[...] = jnp.zeros((ROWS_S, MXU_DIM), F32)

    q = q_ref[...]
    sel = sel_ref[...].astype(BF16)
    erow = lax.broadcasted_iota(jnp.int32, (sel_w, page_size), 0)
    ecol = lax.broadcasted_iota(jnp.int32, (sel_w, page_size), 1)

    def attend(kk, vv, pos, kmask):
        sc = _dot(q, kk)
        km3 = jnp.concatenate([kmask] * R_A, axis=0)
        valid = (km3 > 0.5) & (pos <= tq)
        sc = jnp.where(valid, sc - slope * (float(tq) - pos.astype(F32)), NEG_BIG)
        m_old = m_ref[...]
        m_new = jnp.maximum(m_old, jnp.max(sc, axis=-1, keepdims=True))
        e = jnp.where(valid, jnp.exp(sc - m_new), 0.0)
        alpha = jnp.exp(m_old - m_new)
        l_ref[...] = alpha * l_ref[...] + jnp.sum(e, axis=-1, keepdims=True)
        acc_ref[...] = alpha * acc_ref[...] + _dot_nt(e.astype(BF16), vv)
        m_ref[...] = m_new

    per_page = page_size // SEL_BLOCK
    for k in range(npg):
        pg = j * npg + k
        blk_of_key = pg * per_page + lax.shift_right_logical(ecol, 6)
        expand = jnp.where(erow == 4 * blk_of_key, 1.0, 0.0).astype(BF16)
        kmask = _dot(sel, expand)
        pos = pg * page_size + lax.broadcasted_iota(jnp.int32, (1, page_size), 1)
        pr = page_refs[k]
        attend(pr[0:D_KV_A, :].astype(BF16), pr[D_KV_A:, :].astype(BF16), pos, kmask)

    @pl.when(j == n_steps - 1)
    def _():
        knew = new_ref[:, 2 * D_KV_A:3 * D_KV_A].astype(BF16).astype(F32)
        vnew = new_ref[:, 3 * D_KV_A:].astype(BF16).astype(F32)
        blk_new = tq // SEL_BLOCK
        sel_new = sel_ref[:, 4 * blk_new:4 * blk_new + 1]
        ok = jnp.concatenate([sel_new] * R_A, axis=0) > 0.5
        s_new = jnp.where(ok, jnp.sum(q.astype(F32) * knew, axis=-1, keepdims=True), NEG_BIG)
        m_old = m_ref[...]
        m_new = jnp.maximum(m_old, s_new)
        e_new = jnp.where(ok, jnp.exp(s_new - m_new), 0.0)
        alpha = jnp.exp(m_old - m_new)
        l_fin = alpha * l_ref[...] + e_new
        acc = alpha * acc_ref[...] + e_new.astype(BF16).astype(F32) * vnew
        o_s = acc / jnp.maximum(l_fin, 1e-30)

        grow = gate_ref[...]
        lane_g = lax.broadcasted_iota(jnp.int32, (ROWS_S, LANE), 1)
        col0 = 3 * (R_A * g_of_row + r_of_row)
        gsel = lambda c: jnp.sum(jnp.where(lane_g == col0 + c, grow, 0.0), axis=-1, keepdims=True)
        merged = gsel(0) * oc_ref[...] + gsel(1) * o_s + gsel(2) * ow_ref[...]
        merged = jnp.where(own_lanes & (g_of_row < G_A), merged, 0.0)
        for r in range(R_A):
            out_ref[:, r * MXU_DIM:(r + 1) * MXU_DIM] = jnp.sum(
                merged[r * SUBLANE:(r + 1) * SUBLANE], axis=0, keepdims=True)


def _nsa_sample(page_table, cache_t, layer, qa, gates, nsa_new, win_new, win_new_t, kc, vc, win_t, past_len):
    n_seq, n_pages = page_table.shape
    page_size = cache_t.shape[-1]
    npg = NSA_PAGES_PER_STEP
    n_steps = n_pages // npg
    wb = win_t.shape[-1]
    n_sel = (past_len + 1 + SEL_BLOCK - 1) // SEL_BLOCK
    sel_w = -(-(4 * n_sel) // LANE) * LANE
    tok = lambda w: pl.BlockSpec((None, 1, w), lambda b, j, pt: (b, 0, 0))

    def page_spec(k):
        return pl.BlockSpec((None, None, 2 * D_KV_A, page_size), lambda b, j, pt: (layer, pt[b, j * npg + k], 1, 0))

    cmp_spec = pl.BlockSpec((None, kc.shape[1], D_KV_A), lambda b, j, pt: (b, 0, 0))
    win_in_spec = pl.BlockSpec((None, None, 2 * D_KV_A, wb), lambda b, j, pt: (layer, b, 0, 0))
    win_out_spec = pl.BlockSpec((None, 2 * D_KV_A, wb), lambda b, j, pt: (b, 0, 0))
    newt_spec = pl.BlockSpec((2 * D_KV_A, n_seq), lambda b, j, pt: (0, 0))
    grid_spec = pltpu.PrefetchScalarGridSpec(
        num_scalar_prefetch=1,
        grid=(n_seq, n_steps),
        in_specs=[page_spec(k) for k in range(npg)]
        + [tok(D_A), tok(LANE), tok(4 * D_KV_A), tok(2 * D_KV_A), newt_spec, cmp_spec, cmp_spec, win_in_spec],
        out_specs=[tok(D_A), win_out_spec],
        scratch_shapes=[pltpu.VMEM((ROWS_S, MXU_DIM), BF16), pltpu.VMEM((SUBLANE, sel_w), F32),
                        pltpu.VMEM((ROWS_S, 1), F32), pltpu.VMEM((ROWS_S, 1), F32),
                        pltpu.VMEM((ROWS_S, MXU_DIM), F32), pltpu.VMEM((ROWS_S, MXU_DIM), F32),
                        pltpu.VMEM((ROWS_S, MXU_DIM), F32)],
    )
    r3 = lambda a: a.reshape(n_seq, 1, a.shape[-1])
    oa, win_out = pl.pallas_call(
        functools.partial(_nsa_sample_kernel, past_len=past_len, page_size=page_size),
        grid_spec=grid_spec,
        out_shape=[jax.ShapeDtypeStruct((n_seq, 1, D_A), F32), jax.ShapeDtypeStruct((n_seq, 2 * D_KV_A, wb), F32)],
        compiler_params=_cparams(("arbitrary", "arbitrary")),
        name="nsa_sample",
    )(page_table, *([cache_t] * npg), r3(qa), r3(gates), r3(nsa_new), r3(win_new), win_new_t, kc, vc, win_t)
    return oa.reshape(n_seq, D_A), win_out


ROWS_B = 16
SB_PAGES_PER_STEP = 8


def _sb_sample_kernel(pt_ref, *refs, past_len, page_size, n_pages):
    npg = SB_PAGES_PER_STEP
    page_refs = refs[:npg]
    q_ref, new_ref, u_ref, out_ref, qm_ref, carry_ref, acc_ref = refs[npg:]
    j = pl.program_id(1)
    n_steps = pl.num_programs(1)
    tq = past_len
    rowi = lax.broadcasted_iota(jnp.int32, (ROWS_B, D_B), 0)
    lane = lax.broadcasted_iota(jnp.int32, (ROWS_B, D_B), 1)
    own = lax.shift_right_logical(lane, 6) == rowi

    @pl.when(j == 0)
    def _():
        qm = jnp.where(own, q_ref[...] * QK_SCALE, 0.0)
        qm_ref[...] = qm.astype(BF16)
        acc_ref[...] = jnp.zeros(acc_ref.shape, F32)
        knew = new_ref[:, 0:D_B].astype(BF16).astype(F32)
        z = jnp.sum(qm.astype(BF16).astype(F32) * knew, axis=-1, keepdims=True)
        valid = jnp.full((ROWS_B, 1), past_len < tq)
        lf = jnp.where(valid, _log_sigmoid_neg(z), 0.0)
        a = jnp.where(valid, jnp.exp(_log_sigmoid_neg(z) + z), 0.0)
        acc_ref[...] = a.astype(BF16).astype(F32) * new_ref[:, D_B:].astype(BF16).astype(F32)
        carry_ref[...] = lf

    qm = qm_ref[...]
    u = u_ref[...]
    for k in range(npg):
        pg = (n_pages - 1) - (j * npg + k)
        pr = page_refs[k]
        kk = pr[0:D_B, :].astype(BF16)
        vv = pr[D_B:, :].astype(BF16)
        tk = pg * page_size + lax.broadcasted_iota(jnp.int32, (1, page_size), 1)
        valid = tk < tq
        z = _dot(qm, kk)
        lf_all = _log_sigmoid_neg(z)
        lf = jnp.where(valid, lf_all, 0.0)
        later = carry_ref[...] + _split_dot(lf, u)
        a = jnp.where(valid, jnp.exp(lf_all + z + later), 0.0)
        acc_ref[...] = acc_ref[...] + _dot_nt(a.astype(BF16), vv)
        carry_ref[...] = carry_ref[...] + jnp.sum(lf, axis=-1, keepdims=True)

    @pl.when(j == n_steps - 1)
    def _():
        out_ref[...] = jnp.sum(jnp.where(own, acc_ref[...], 0.0), axis=0, keepdims=True)


def _sb_sample(page_table, cache_t, layer, qb, sb_new, umat, past_len):
    n_seq, n_pages = page_table.shape
    page_size = cache_t.shape[-1]
    npg = SB_PAGES_PER_STEP
    n_steps = n_pages // npg
    tok = lambda w: pl.BlockSpec((None, 1, w), lambda b, j, pt: (b, 0, 0))

    def page_spec(k):
        return pl.BlockSpec((None, None, 2 * D_B, page_size),
                            lambda b, j, pt: (layer, pt[b, n_pages - 1 - (j * npg + k)], 0, 0))

    grid_spec = pltpu.PrefetchScalarGridSpec(
        num_scalar_prefetch=1,
        grid=(n_seq, n_steps),
        in_specs=[page_spec(k) for k in range(npg)]
        + [tok(D_B), tok(2 * D_B), pl.BlockSpec((page_size, page_size), lambda b, j, pt: (0, 0))],
        out_specs=tok(D_B),
        scratch_shapes=[pltpu.VMEM((ROWS_B, D_B), BF16), pltpu.VMEM((ROWS_B, 1), F32), pltpu.VMEM((ROWS_B, D_B), F32)],
    )
    r3 = lambda a: a.reshape(n_seq, 1, a.shape[-1])
    out = pl.pallas_call(
        functools.partial(_sb_sample_kernel, past_len=past_len, page_size=page_size, n_pages=n_pages),
        grid_spec=grid_spec,
        out_shape=jax.ShapeDtypeStruct((n_seq, 1, D_B), F32),
        compiler_params=_cparams(("arbitrary", "arbitrary")),
        name="sb_sample",
    )(page_table, *([cache_t] * npg), r3(qb), r3(sb_new), umat)
    return out.reshape(n_seq, D_B)


def _oa_perm():
    perm = np.zeros((D_A,), np.int32)
    for r in range(R_A):
        for g in range(G_A):
            for d in range(HEAD_DIM):
                perm[r * MXU_DIM + g * HEAD_DIM + d] = g * R_A * HEAD_DIM + r * HEAD_DIM + d
    return perm


def _place_mats():
    place = np.zeros((H_A, LANE, MXU_DIM), np.float32)
    for g in range(G_A):
        for r in range(R_A):
            h = R_A * g + r
            for d in range(HEAD_DIM):
                place[h, (h % 2) * HEAD_DIM + d, g * HEAD_DIM + d] = 1.0
    return place


def _expand_mat(t):
    e = np.zeros((Q_TILE, t), np.float32)
    for k in range(t):
        e[4 * (k // SEL_BLOCK), k] = 1.0
    return e


def _prep_layer(l, w_in, w_out, norm_mix, norm_ffn, q_norm, k_norm, cmp_pos, cmp_w1, cmp_w2,
                out_norm_a, out_norm_b, conv_w, conv_b, conv_ln_g, conv_ln_b, w_gate, w_up, w_down):
    wt = jnp.transpose(w_in[l])
    g0 = D_A + 6 * D_KV_A
    w_in_t = jnp.concatenate(
        [wt[:g0], wt[g0 + N_GATE:], wt[g0:g0 + N_GATE], jnp.zeros((LANE - N_GATE, D_MODEL), F32)],
        axis=0).astype(BF16)
    perm = _oa_perm()
    wo = w_out[l]
    w_out_p = jnp.concatenate([wo[:D_A][perm], wo[D_A:]], axis=0).astype(BF16)
    eye = jnp.eye(G_A, dtype=F32)
    w1 = cmp_w1[l].reshape(2, CMP_LEN, HEAD_DIM, HEAD_DIM)
    w1bd = jnp.einsum('gh,kpde->kpgdhe', eye, w1).reshape(2, CMP_LEN, D_KV_A, D_KV_A).astype(BF16)
    w2bd = jnp.einsum('gh,kde->kgdhe', eye, cmp_w2[l]).reshape(2, D_KV_A, D_KV_A).astype(BF16)
    tile4 = lambda v: jnp.tile(v, G_A).reshape(1, D_KV_A)
    col4 = lambda v: jnp.tile(v, G_A).reshape(D_KV_A, 1)
    return dict(
        w_in_t=w_in_t, w_out_p=w_out_p, w1bd=w1bd, w2bd=w2bd,
        pos4=jnp.tile(cmp_pos[l], (1, 1, G_A)),
        gn_mix=norm_mix[l].reshape(1, D_MODEL), gn_ffn=norm_ffn[l].reshape(1, D_MODEL),
        qn4=tile4(q_norm[l]), kn0_4=tile4(k_norm[l, 0]), kn1_4=tile4(k_norm[l, 1]), kn2_4=tile4(k_norm[l, 2]),
        kn1c=col4(k_norm[l, 1]), kn2c=col4(k_norm[l, 2]),
        ga=out_norm_a[l][perm].reshape(1, D_A), gb=out_norm_b[l].reshape(1, D_B),
        cw=conv_w[l], cb=conv_b[l].reshape(1, D_C), lg=conv_ln_g[l].reshape(1, D_C), lb=conv_ln_b[l].reshape(1, D_C),
        wg=w_gate[l].astype(BF16), wu=w_up[l].astype(BF16), wd=w_down[l].astype(BF16),
    )


def kernel(x_prompt, x_sample, cache_nsa, cache_sb, state_win, state_conv, page_table, w_in, w_out, norm_mix,
           norm_ffn, q_norm, k_norm, cmp_pos, cmp_w1, cmp_w2, out_norm_a, out_norm_b, conv_w, conv_b, conv_ln_g,
           conv_ln_b, w_gate, w_up, w_down):
    b, t, _ = x_prompt.shape
    n_seq, dec_seq, _ = x_sample.shape
    assert dec_seq == 1 and t % SB_TILE == 0 and t % (CMP_STRIDE * SUBLANE) == 0 and t >= WINDOW
    n_pool, page_size = cache_nsa.shape[1], cache_nsa.shape[2]
    n_pages = page_table.shape[1]
    past_len = n_pages * page_size
    assert page_size % CMP_STRIDE == 0 and page_size // CMP_STRIDE == SUBLANE
    assert n_pages % CMP_PAGES_PER_STEP == 0 and n_pages % NSA_PAGES_PER_STEP == 0 and n_pages % SB_PAGES_PER_STEP == 0

    pmat = jnp.asarray(np.kron(np.eye(G_A, dtype=np.float32), np.full((HEAD_DIM, HEAD_DIM), 1.0 / HEAD_DIM, np.float32)), BF16)
    place = jnp.asarray(_place_mats(), BF16)
    expand = jnp.asarray(_expand_mat(t), BF16)
    u_prompt = jnp.asarray(np.tril(np.ones((SB_TILE, SB_TILE), np.float32), -1), BF16)
    u_page = jnp.asarray(np.tril(np.ones((page_size, page_size), np.float32), -1), BF16)

    fmaj = lambda a: jnp.transpose(a, (0, 1, 3, 4, 5, 2))
    cache_nsa_t = fmaj(cache_nsa).reshape(DEPTH, n_pool, 4 * D_KV_A, page_size)
    cache_sb_t = fmaj(cache_sb).reshape(DEPTH, n_pool, 2 * D_B, page_size)
    wb = state_win.shape[2]
    win_t = fmaj(state_win).reshape(DEPTH, n_seq, 2 * D_KV_A, wb)
    conv_t = jnp.transpose(state_conv, (0, 2, 1, 3))
    tmaj = lambda a, lead: jnp.transpose(a.reshape(a.shape[0], *lead, a.shape[-1]), (0, len(lead) + 1, *range(1, len(lead) + 1)))

    xp = x_prompt.reshape(b * t, D_MODEL)
    xs = x_sample.reshape(n_seq, D_MODEL)
    outs = {k: [] for k in ("p_nsa", "p_sb", "p_win", "p_conv", "s_nsa", "s_sb", "s_win", "s_conv")}
    for l in range(DEPTH):
        w = _prep_layer(l, w_in, w_out, norm_mix, norm_ffn, q_norm, k_norm, cmp_pos, cmp_w1, cmp_w2, out_norm_a,
                        out_norm_b, conv_w, conv_b, conv_ln_g, conv_ln_b, w_gate, w_up, w_down)

        qa, nsa_t, winp_t, qb, sb_t, cin, gates, cmp_rows = _in_proj_prompt(
            xp, w["gn_mix"], w["w_in_t"], w["qn4"], w["kn1c"], w["kn2c"], pmat, b, t)
        kc, vc = _compress_prompt(cmp_rows.reshape(b, t // CMP_STRIDE, CMP_STRIDE * 2 * D_KV_A),
                                  w["pos4"], w["w1bd"], w["w2bd"], w["kn0_4"], pmat)
        oa = _nsa_prompt(qa, gates, kc, vc, nsa_t, winp_t, place, expand, b, t)
        ob = _sb_prompt(qb, sb_t, u_prompt, b, t)
        cv, conv_state = _conv_prompt(cin, w["cw"], w["cb"], w["lg"], w["lb"], b, t)
        xp = _out_proj(oa, ob, cv, xp, w["ga"], w["gb"], w["w_out_p"], tm=512)
        xp = _ffn(xp, w["gn_ffn"], w["wg"], w["wu"], w["wd"], tm=1024, tf=512, tn=512)
        outs["p_nsa"].append(tmaj(nsa_t, (4, G_A, HEAD_DIM)))
        outs["p_sb"].append(tmaj(sb_t, (2, H_B, HEAD_DIM)))
        outs["p_win"].append(tmaj(winp_t[:, :, t - min(WINDOW, t):], (2, G_A, HEAD_DIM)))
        outs["p_conv"].append(conv_state)

        qa, nsa_new, win_new, qb, sb_new, cin, gates, win_new_t = _in_proj_sample(
            xs, w["gn_mix"], w["w_in_t"], w["qn4"], w["kn1_4"], w["kn2_4"], w["kn2c"], pmat)
        kc, vc = _compress_sample(page_table, cache_nsa_t, l, w["pos4"], w["w1bd"], w["w2bd"], w["kn0_4"], pmat)
        oa, win_out_t = _nsa_sample(page_table, cache_nsa_t, l, qa, gates, nsa_new, win_new, win_new_t, kc, vc,
                                    win_t, past_len)
        ob = _sb_sample(page_table, cache_sb_t, l, qb, sb_new, u_page, past_len)
        cv, conv_state_t = _conv_sample(cin, conv_t, l, w["cw"], w["cb"], w["lg"], w["lb"])
        xs = _out_proj(oa, ob, cv, xs, w["ga"], w["gb"], w["w_out_p"], tm=n_seq)
        xs = _ffn(xs, w["gn_ffn"], w["wg"], w["wu"], w["wd"], tm=n_seq, tf=512, tn=512)
        outs["s_nsa"].append(nsa_new.reshape(n_seq, 1, 4, G_A, HEAD_DIM))
        outs["s_sb"].append(sb_new.reshape(n_seq, 1, 2, H_B, HEAD_DIM))
        outs["s_win"].append(tmaj(win_out_t, (2, G_A, HEAD_DIM)))
        outs["s_conv"].append(jnp.transpose(conv_state_t, (1, 0, 2)))

    st = lambda k: jnp.stack(outs[k])
    return (xp.reshape(b, t, D_MODEL), xs.reshape(n_seq, 1, D_MODEL), st("p_nsa"), st("p_sb"), st("p_win"),
            st("p_conv"), st("s_nsa"), st("s_sb"), st("s_win"), st("s_conv"))
```

```python
import functools

import numpy as np
import jax
import jax.numpy as jnp
from jax import lax
from jax.experimental import pallas as pl
from jax.experimental.pallas import tpu as pltpu

F32 = jnp.float32
BF16 = jnp.bfloat16

LANE = 128
SUBLANE = 8
MXU_DIM = 256
VMEM_LIMIT_BYTES = 56 * 1024 * 1024

D_MODEL = 2048
DEPTH = 2
HEAD_DIM = 64
H_A = 12
G_A = 4
R_A = H_A // G_A
H_B = 12
D_A = H_A * HEAD_DIM
D_KV_A = G_A * HEAD_DIM
D_B = H_B * HEAD_DIM
D_C = D_MODEL - D_A - D_B
CONV_W = 31
CMP_LEN = 32
CMP_STRIDE = 16
SEL_BLOCK = 64
TOP_N = 16
WINDOW = 512
FORCE_BONUS = 1000.0
N_GATE = 3 * H_A
D_IN = D_A + 6 * D_KV_A + N_GATE + 3 * D_B + 2 * D_C
D_FF = -(-8 * D_MODEL // (3 * 256)) * 256
EPS = 1e-6
QK_SCALE = HEAD_DIM ** -0.5
NEG_BIG = -1e30

S_KV = D_A
S_GATE = S_KV + 6 * D_KV_A
S_QB = S_GATE + N_GATE
S_KB = S_QB + D_B
S_CIN = S_KB + 2 * D_B
T_QA = 0
T_QB = T_QA + D_A
T_CIN = T_QB + D_B
T_GATE = T_CIN + 2 * D_C
T_CMP = T_GATE + LANE
W_TOK = T_CMP + 2 * D_KV_A
F_KV = 0
F_SB = F_KV + 6 * D_KV_A
W_FEAT = F_SB + 2 * D_B

SLOPES = (2.0 ** (-8.0 * np.arange(1, H_A + 1, dtype=np.float32) / H_A)).astype(np.float32).reshape(G_A, R_A)


def _cparams(sem):
    return pltpu.CompilerParams(dimension_semantics=sem, vmem_limit_bytes=VMEM_LIMIT_BYTES)


def _const_spec(shape):
    nd = len(shape)
    return pl.BlockSpec(shape, lambda *_: (0,) * nd)


def _dot(a, b):
    return jnp.dot(a, b, preferred_element_type=F32)


def _dot_nt(a, b):
    return lax.dot_general(a, b, (((1,), (1,)), ((), ())), preferred_element_type=F32)


def _split_dot(x, w_bf16):
    hi = x.astype(BF16)
    lo = (x - hi.astype(F32)).astype(BF16)
    return _dot(hi, w_bf16) + _dot(lo, w_bf16)


def _sigmoid(x):
    return 1.0 / (1.0 + jnp.exp(-x))


def _rmsnorm_rows(x, gain):
    ms = jnp.mean(x * x, axis=-1, keepdims=True)
    return x * lax.rsqrt(ms + EPS) * gain


def _head_rmsnorm(u, gain_tile, pmat):
    outs = []
    for c in range(u.shape[1] // MXU_DIM):
        uc = u[:, c * MXU_DIM:(c + 1) * MXU_DIM]
        ms = _split_dot(uc * uc, pmat)
        outs.append(uc * lax.rsqrt(ms + EPS) * gain_tile)
    return outs[0] if len(outs) == 1 else jnp.concatenate(outs, axis=-1)


def _masked_softmax(s, mask):
    s = jnp.where(mask, s, -jnp.inf)
    m = jnp.max(s, axis=-1, keepdims=True)
    m = jnp.where(m == -jnp.inf, 0.0, m)
    e = jnp.where(mask, jnp.exp(s - m), 0.0)
    return e / jnp.maximum(jnp.sum(e, axis=-1, keepdims=True), 1e-30)


def _log_sigmoid_neg(z):
    return -(jnp.maximum(z, 0.0) + jnp.log(1.0 + jnp.exp(-jnp.abs(z))))


def _gelu_tanh(x):
    return 0.5 * x * (1.0 + jnp.tanh(np.float32(np.sqrt(2.0 / np.pi)) * (x + 0.044715 * (x * x * x))))


def _head_rmsnorm_t(u, gain_col, pmat):
    sq = u * u
    hi = sq.astype(BF16)
    lo = (sq - hi.astype(F32)).astype(BF16)
    ms = _dot(pmat, hi) + _dot(pmat, lo)
    return u * lax.rsqrt(ms + EPS) * gain_col


def _in_proj_prompt_kernel(x_ref, gn_ref, wtok_ref, wfeat_ref, qn_ref, kn1c_ref, kn2c_ref, pmat_ref,
                           qa_ref, nsat_ref, wint_ref, qb_ref, sbt_ref, cin_ref, gate_ref, cmp_ref):
    xn = _rmsnorm_rows(x_ref[...], gn_ref[...]).astype(BF16)
    pmat = pmat_ref[...]
    tok = lambda lo, hi: _dot(xn, wtok_ref[:, lo:hi])
    feat = lambda lo, hi: _dot_nt(wfeat_ref[lo:hi, :], xn)

    qa_ref[...] = _head_rmsnorm(tok(T_QA, T_QB), qn_ref[...], pmat)
    kvt = feat(F_KV, F_SB)
    nsat_ref[0:2 * D_KV_A, :] = kvt[0:2 * D_KV_A]
    nsat_ref[2 * D_KV_A:3 * D_KV_A, :] = _head_rmsnorm_t(kvt[2 * D_KV_A:3 * D_KV_A], kn1c_ref[...], pmat)
    nsat_ref[3 * D_KV_A:4 * D_KV_A, :] = kvt[3 * D_KV_A:4 * D_KV_A]
    wint_ref[0:D_KV_A, :] = _head_rmsnorm_t(kvt[4 * D_KV_A:5 * D_KV_A], kn2c_ref[...], pmat)
    wint_ref[D_KV_A:2 * D_KV_A, :] = kvt[5 * D_KV_A:6 * D_KV_A]
    qb_ref[...] = tok(T_QB, T_CIN)
    sbt_ref[...] = feat(F_SB, W_FEAT)
    cin_ref[...] = tok(T_CIN, T_GATE)
    gate_ref[...] = _sigmoid(tok(T_GATE, T_CMP))
    cmp_ref[...] = tok(T_CMP, W_TOK)


IN_PROJ_TILE = 256


def _resident_spec(shape):
    return pl.BlockSpec(shape, lambda *_: (0,) * len(shape), pipeline_mode=pl.Buffered(1))


def _in_proj_prompt(x2d, gn, w_tok, w_feat, qn4, kn1c, kn2c, pmat, b, t):
    tm = IN_PROJ_TILE
    nt = t // tm
    row = lambda w: pl.BlockSpec((tm, w), lambda i: (i, 0))
    fmaj = lambda f: pl.BlockSpec((None, f, tm), lambda i: (i // nt, 0, i % nt))
    tok_out = lambda w: jax.ShapeDtypeStruct((b * t, w), F32)
    feat_out = lambda f: jax.ShapeDtypeStruct((b, f, t), F32)
    col = _const_spec((MXU_DIM, 1))
    return pl.pallas_call(
        _in_proj_prompt_kernel,
        grid=(b * nt,),
        in_specs=[row(D_MODEL), _const_spec((1, D_MODEL)),
                  _resident_spec((D_MODEL, W_TOK)), _resident_spec((W_FEAT, D_MODEL)),
                  _const_spec((1, MXU_DIM)), col, col, _const_spec((MXU_DIM, MXU_DIM))],
        out_specs=[row(D_A), fmaj(4 * D_KV_A), fmaj(2 * D_KV_A), row(D_B), fmaj(2 * D_B), row(2 * D_C), row(LANE),
                   row(2 * D_KV_A)],
        out_shape=[tok_out(D_A), feat_out(4 * D_KV_A), feat_out(2 * D_KV_A), tok_out(D_B), feat_out(2 * D_B),
                   tok_out(2 * D_C), tok_out(LANE), tok_out(2 * D_KV_A)],
        compiler_params=_cparams(("arbitrary",)),
        name="in_proj_prompt",
    )(x2d, gn, w_tok, w_feat, qn4, kn1c, kn2c, pmat)


def _in_proj_sample_kernel(x_ref, gn_ref, wtok_ref, wfeat_ref, qn_ref, kn1_ref, kn2_ref, kn2c_ref, pmat_ref,
                           qa_ref, nsa_ref, win_ref, qb_ref, sb_ref, cin_ref, gate_ref, wint_ref):
    xn = _rmsnorm_rows(x_ref[...], gn_ref[...]).astype(BF16)
    pmat = pmat_ref[...]
    tok = lambda lo, hi: _dot(xn, wtok_ref[:, lo:hi])
    tok_f = lambda lo, hi: _dot_nt(xn, wfeat_ref[lo:hi, :])

    qa_ref[...] = _head_rmsnorm(tok(T_QA, T_QB), qn_ref[...], pmat)
    kv = tok_f(F_KV, F_SB)
    nsa_ref[:, 0:2 * D_KV_A] = kv[:, 0:2 * D_KV_A]
    nsa_ref[:, 2 * D_KV_A:3 * D_KV_A] = _head_rmsnorm(kv[:, 2 * D_KV_A:3 * D_KV_A], kn1_ref[...], pmat)
    nsa_ref[:, 3 * D_KV_A:4 * D_KV_A] = kv[:, 3 * D_KV_A:4 * D_KV_A]
    win_ref[:, 0:D_KV_A] = _head_rmsnorm(kv[:, 4 * D_KV_A:5 * D_KV_A], kn2_ref[...], pmat)
    win_ref[:, D_KV_A:2 * D_KV_A] = kv[:, 5 * D_KV_A:6 * D_KV_A]
    qb_ref[...] = tok(T_QB, T_CIN)
    sb_ref[...] = tok_f(F_SB, W_FEAT)
    cin_ref[...] = tok(T_CIN, T_GATE)
    gate_ref[...] = _sigmoid(tok(T_GATE, T_CMP))
    wt_win = _dot_nt(wfeat_ref[F_KV + 4 * D_KV_A:F_SB, :], xn)
    wint_ref[0:D_KV_A, :] = _head_rmsnorm_t(wt_win[0:D_KV_A], kn2c_ref[...], pmat)
    wint_ref[D_KV_A:, :] = wt_win[D_KV_A:]


def _in_proj_sample(x2d, gn, w_tok, w_feat, qn4, kn1_4, kn2_4, kn2c, pmat):
    m = x2d.shape[0]
    widths = (D_A, 4 * D_KV_A, 2 * D_KV_A, D_B, 2 * D_B, 2 * D_C, LANE)
    full = lambda r, c: _const_spec((r, c))
    return pl.pallas_call(
        _in_proj_sample_kernel,
        grid=(1,),
        in_specs=[full(m, D_MODEL), full(1, D_MODEL),
                  _resident_spec((D_MODEL, W_TOK)), _resident_spec((W_FEAT, D_MODEL)),
                  full(1, MXU_DIM), full(1, MXU_DIM), full(1, MXU_DIM), full(MXU_DIM, 1), full(MXU_DIM, MXU_DIM)],
        out_specs=[full(m, w) for w in widths] + [full(2 * D_KV_A, m)],
        out_shape=[jax.ShapeDtypeStruct((m, w), F32) for w in widths]
        + [jax.ShapeDtypeStruct((2 * D_KV_A, m), F32)],
        compiler_params=_cparams(("arbitrary",)),
        name="in_proj_sample",
    )(x2d, gn, w_tok, w_feat, qn4, kn1_4, kn2_4, kn2c, pmat)


def _out_proj_kernel(oa_ref, ob_ref, cv_ref, x_ref, ga_ref, gb_ref, w_ref, y_ref):
    ha = _rmsnorm_rows(oa_ref[...], ga_ref[...]).astype(BF16)
    hb = _rmsnorm_rows(ob_ref[...], gb_ref[...]).astype(BF16)
    hc = cv_ref[...].astype(BF16)
    acc = _dot(ha, w_ref[0:D_A, :]) + _dot(hb, w_ref[D_A:D_A + D_B, :]) + _dot(hc, w_ref[D_A + D_B:, :])
    y_ref[...] = x_ref[...] + acc


def _out_proj(oa, ob, cv, x2d, ga, gb, w_out_p, tm):
    m = x2d.shape[0]
    row = lambda w: pl.BlockSpec((tm, w), lambda i: (i, 0))
    return pl.pallas_call(
        _out_proj_kernel,
        grid=(m // tm,),
        in_specs=[row(D_A), row(D_B), row(D_C), row(D_MODEL), _const_spec((1, D_A)), _const_spec((1, D_B)),
                  _const_spec((D_MODEL, D_MODEL))],
        out_specs=row(D_MODEL),
        out_shape=jax.ShapeDtypeStruct((m, D_MODEL), F32),
        compiler_params=_cparams(("arbitrary",)),
        name="out_proj",
    )(oa, ob, cv, x2d, ga, gb, w_out_p)


def _ffn_up_kernel(x_ref, gn_ref, wg_ref, wu_ref, h_ref, xn_ref):
    @pl.when(pl.program_id(1) == 0)
    def _():
        xn_ref[...] = _rmsnorm_rows(x_ref[...], gn_ref[...]).astype(BF16)

    xn = xn_ref[...]
    a = _dot(xn, wg_ref[...])
    b = _dot(xn, wu_ref[...])
    h_ref[...] = (a * _sigmoid(a) * b).astype(BF16)


def _ffn_down_kernel(h_ref, x_ref, wd_ref, y_ref):
    y_ref[...] = x_ref[...] + _dot(h_ref[...], wd_ref[...])


def _ffn(x2d, gn, wg, wu, wd, tm, tf, tn):
    m = x2d.shape[0]
    h = pl.pallas_call(
        _ffn_up_kernel,
        grid=(m // tm, D_FF // tf),
        in_specs=[pl.BlockSpec((tm, D_MODEL), lambda i, j: (i, 0)), _const_spec((1, D_MODEL)),
                  pl.BlockSpec((D_MODEL, tf), lambda i, j: (0, j)), pl.BlockSpec((D_MODEL, tf), lambda i, j: (0, j))],
        out_specs=pl.BlockSpec((tm, tf), lambda i, j: (i, j)),
        out_shape=jax.ShapeDtypeStruct((m, D_FF), BF16),
        scratch_shapes=[pltpu.VMEM((tm, D_MODEL), BF16)],
        compiler_params=_cparams(("arbitrary", "arbitrary")),
        name="ffn_up",
    )(x2d, gn, wg, wu)
    return pl.pallas_call(
        _ffn_down_kernel,
        grid=(m // tm, D_MODEL // tn),
        in_specs=[pl.BlockSpec((tm, D_FF), lambda i, j: (i, 0)), pl.BlockSpec((tm, tn), lambda i, j: (i, j)),
                  pl.BlockSpec((D_FF, tn), lambda i, j: (0, j))],
        out_specs=pl.BlockSpec((tm, tn), lambda i, j: (i, j)),
        out_shape=jax.ShapeDtypeStruct((m, D_MODEL), F32),
        compiler_params=_cparams(("arbitrary", "arbitrary")),
        name="ffn_down",
    )(h, x2d, wd)


def _compress_rows(get_piece, n_rows, pos_ref, w1_ref, w2_ref, kn0_ref, pmat_ref):
    outs = []
    for kv in range(2):
        first = jnp.zeros((n_rows, MXU_DIM), F32)
        second = jnp.zeros((n_rows, MXU_DIM), F32)
        for p in range(CMP_STRIDE):
            xp = get_piece(p, kv)
            first = first + _dot((xp + pos_ref[kv, p:p + 1, :]).astype(BF16), w1_ref[kv, p])
            q = CMP_STRIDE + p
            second = second + _dot((xp + pos_ref[kv, q:q + 1, :]).astype(BF16), w1_ref[kv, q])
        pre = first + pltpu.roll(second, n_rows - 1, axis=0)
        o = _dot(_gelu_tanh(pre).astype(BF16), w2_ref[kv])
        if kv == 0:
            o = _head_rmsnorm(o, kn0_ref[...], pmat_ref[...])
        outs.append(o)
    return outs


def _compress_prompt_kernel(x_ref, pos_ref, w1_ref, w2_ref, kn0_ref, pmat_ref, kc_ref, vc_ref):
    n_rows = x_ref.shape[0]
    row_w = 2 * D_KV_A

    def piece(p, kv):
        lo = p * row_w + kv * D_KV_A
        return x_ref[:, lo:lo + D_KV_A]

    kc, vc = _compress_rows(piece, n_rows, pos_ref, w1_ref, w2_ref, kn0_ref, pmat_ref)
    kc_ref[...] = kc
    vc_ref[...] = vc


def _compress_prompt(nsa16, pos4, w1bd, w2bd, kn0_4, pmat):
    b, nsub, w = nsa16.shape
    return pl.pallas_call(
        _compress_prompt_kernel,
        grid=(b,),
        in_specs=[pl.BlockSpec((None, nsub, w), lambda i: (i, 0, 0)),
                  _const_spec(pos4.shape), _const_spec(w1bd.shape), _const_spec(w2bd.shape),
                  _const_spec((1, MXU_DIM)), _const_spec((MXU_DIM, MXU_DIM))],
        out_specs=[pl.BlockSpec((None, nsub, D_KV_A), lambda i: (i, 0, 0))] * 2,
        out_shape=[jax.ShapeDtypeStruct((b, nsub, D_KV_A), F32)] * 2,
        compiler_params=_cparams(("arbitrary",)),
        name="compress_prompt",
    )(nsa16, pos4, w1bd, w2bd, kn0_4, pmat)


CMP_PAGES_PER_STEP = 32


def _compress_sample_kernel(pt_ref, *refs, page_size):
    npg = CMP_PAGES_PER_STEP
    page_refs = refs[:npg + 1]
    pos_ref, w1_ref, w2_ref, kn0_ref, pmat_ref, kc_ref, vc_ref, rows_ref = refs[npg + 1:]
    sub = page_size // CMP_STRIDE
    n_rows = (npg + 1) * sub
    n_slab = 2 * D_KV_A // LANE

    for k, pr in enumerate(page_refs):
        for s in range(n_slab):
            rows_ref[s, k * page_size:(k + 1) * page_size, :] = pr[s * LANE:(s + 1) * LANE, :].T

    def piece(p, kv):
        slabs = [rows_ref[2 * kv + h, pl.ds(p, n_rows, stride=CMP_STRIDE), :] for h in range(2)]
        return jnp.concatenate(slabs, axis=-1)

    kc, vc = _compress_rows(piece, n_rows, pos_ref, w1_ref, w2_ref, kn0_ref, pmat_ref)
    kc_ref[...] = kc[0:npg * sub]
    vc_ref[...] = vc[0:npg * sub]


def _compress_sample(page_table, cache_t, layer, pos4, w1bd, w2bd, kn0_4, pmat):
    n_seq, n_pages = page_table.shape
    page_size = cache_t.shape[-1]
    npg = CMP_PAGES_PER_STEP
    n_steps = n_pages // npg
    sub = page_size // CMP_STRIDE

    def page_spec(k):
        def imap(b, j, pt):
            return (layer, pt[b, jnp.minimum(j * npg + k, n_pages - 1)], 0, 0)
        return pl.BlockSpec((None, None, 2 * D_KV_A, page_size), imap)

    const = lambda shape: pl.BlockSpec(shape, lambda b, j, pt: (0,) * len(shape))
    out_spec = pl.BlockSpec((None, npg * sub, D_KV_A), lambda b, j, pt: (b, j, 0))
    grid_spec = pltpu.PrefetchScalarGridSpec(
        num_scalar_prefetch=1,
        grid=(n_seq, n_steps),
        in_specs=[page_spec(k) for k in range(npg + 1)]
        + [const(pos4.shape), const(w1bd.shape), const(w2bd.shape), const((1, MXU_DIM)), const((MXU_DIM, MXU_DIM))],
        out_specs=[out_spec, out_spec],
        scratch_shapes=[pltpu.VMEM((2 * D_KV_A // LANE, (npg + 1) * page_size, LANE), F32)],
    )
    return pl.pallas_call(
        functools.partial(_compress_sample_kernel, page_size=page_size),
        grid_spec=grid_spec,
        out_shape=[jax.ShapeDtypeStruct((n_seq, n_pages * sub, D_KV_A), F32)] * 2,
        compiler_params=_cparams(("arbitrary", "arbitrary")),
        name="compress_sample",
    )(page_table, *([cache_t] * (npg + 1)), pos4, w1bd, w2bd, kn0_4, pmat)


def _select_blocks(score, n_top):
    width = score.shape[1]
    per = SEL_BLOCK // CMP_STRIDE
    lane = lax.broadcasted_iota(jnp.int32, (1, width), 1)
    rank = jnp.zeros(score.shape, F32)
    for k in range(1, width // per):
        other = pltpu.roll(score, per * k, axis=1)
        from_lower = jnp.where(lane >= per * k, 1.0, 0.0)
        rank = rank + jnp.where(other > score, 1.0, jnp.where(other == score, from_lower, 0.0))
    return jnp.where(rank < n_top, jnp.where(score >= 0.0, 1.0, 0.0), 0.0)


def _block_scores(imp, tq_col, n_sel):
    width = imp.shape[1]
    per = SEL_BLOCK // CMP_STRIDE
    pooled = imp
    for k in range(1, per):
        pooled = pooled + pltpu.roll(imp, width - k, axis=1)
    lane = lax.broadcasted_iota(jnp.int32, imp.shape, 1)
    blk = lax.shift_right_logical(lane, 2)
    cur = lax.shift_right_logical(tq_col, 6)
    valid = blk <= cur
    forced = (blk == 0) | (blk == cur) | (blk == cur - 1)
    score = jnp.where(valid, jnp.where(forced, FORCE_BONUS, pooled), -1.0)
    cand = ((lane & (per - 1)) == 0) & (blk < n_sel)
    return jnp.where(cand, score, -jnp.inf)


Q_TILE = 128
SEL_CHUNK = 512
N_WIN_BLOCKS = WINDOW // Q_TILE + 1


def _nsa_prompt_kernel(qa_ref, gate_ref, kc_ref, vc_ref, ksv_ref, w0_ref, w1_ref, w2_ref, w3_ref, w4_ref,
                       place_ref, expand_ref, out_ref, km_ref, s_ref, mx_ref, l_ref, acc_ref):
    win_refs = (w0_ref, w1_ref, w2_ref, w3_ref, w4_ref)
    t_len = ksv_ref.shape[1]
    n_cmp = kc_ref.shape[0]
    n_sel = t_len // SEL_BLOCK
    n_chunk_total = t_len // SEL_CHUNK
    qi = pl.program_id(1)
    start = qi * Q_TILE
    rows3 = R_A * Q_TILE
    rowi = lax.broadcasted_iota(jnp.int32, (rows3, 1), 0)
    r_of_row = lax.shift_right_logical(rowi, 7)
    tq3 = start + (rowi & (Q_TILE - 1))
    tq3f = tq3.astype(F32)
    tq1 = tq3[0:Q_TILE]
    lane_out = lax.broadcasted_iota(jnp.int32, (Q_TILE, MXU_DIM), 1)
    n_chunks = lax.shift_right_logical(start + Q_TILE + SEL_CHUNK - 1, 9)

    q3s, slopes, o_cs, scores = [], [], [], []
    kc = kc_ref[...].astype(BF16)
    vc = vc_ref[...].astype(BF16)
    ends = lax.broadcasted_iota(jnp.int32, (1, n_cmp), 1) * CMP_STRIDE + (CMP_LEN - 1)
    for g in range(G_A):
        qs = []
        for r in range(R_A):
            h = R_A * g + r
            qblk = qa_ref[:, (h // 2) * LANE:(h // 2 + 1) * LANE].astype(BF16)
            qs.append(_dot(qblk, place_ref[h]) * QK_SCALE)
        q3 = jnp.concatenate(qs, axis=0).astype(BF16)
        slope3 = jnp.where(r_of_row == 0, float(SLOPES[g, 0]),
                           jnp.where(r_of_row == 1, float(SLOPES[g, 1]), float(SLOPES[g, 2])))
        s = _dot_nt(q3, kc) - slope3 * (tq3f - ends.astype(F32))
        p_c = _masked_softmax(s, ends <= tq3)
        o_cs.append(_dot(p_c.astype(BF16), vc))
        imp = p_c[0:Q_TILE] + p_c[Q_TILE:2 * Q_TILE] + p_c[2 * Q_TILE:3 * Q_TILE]
        scores.append(_block_scores(imp, tq1, n_sel))
        q3s.append(q3)
        slopes.append(slope3)

    sel = _select_blocks(jnp.concatenate(scores, axis=0), min(TOP_N, n_sel))
    km_ref[...] = _dot(sel.astype(BF16), expand_ref[...])

    mx_ref[...] = jnp.full(mx_ref.shape, NEG_BIG, F32)
    l_ref[...] = jnp.zeros(l_ref.shape, F32)
    acc_ref[...] = jnp.zeros(acc_ref.shape, F32)
    for c in range(n_chunk_total):
        lo, hi = c * SEL_CHUNK, (c + 1) * SEL_CHUNK

        @pl.when(c < n_chunks)
        def _(lo=lo, hi=hi):
            kk = ksv_ref[0:D_KV_A, lo:hi].astype(BF16)
            pos = lo + lax.broadcasted_iota(jnp.int32, (1, SEL_CHUNK), 1)
            dist = tq3f - pos.astype(F32)
            causal = pos <= tq3
            for g in range(G_A):
                kmc = km_ref[g * Q_TILE:(g + 1) * Q_TILE, lo:hi]
                valid = (jnp.concatenate([kmc, kmc, kmc], axis=0) > 0.5) & causal
                sc = jnp.where(valid, _dot(q3s[g], kk) - slopes[g] * dist, NEG_BIG)
                s_ref[g, :, lo:hi] = sc
                m = mx_ref[g]
                for k in range(SEL_CHUNK // LANE):
                    m = jnp.maximum(m, sc[:, k * LANE:(k + 1) * LANE])
                mx_ref[g] = m

    row_max = [jnp.max(mx_ref[g], axis=-1, keepdims=True) for g in range(G_A)]
    for c in range(n_chunk_total):
        lo, hi = c * SEL_CHUNK, (c + 1) * SEL_CHUNK

        @pl.when(c < n_chunks)
        def _(lo=lo, hi=hi):
            vv = ksv_ref[D_KV_A:2 * D_KV_A, lo:hi].astype(BF16)
            for g in range(G_A):
                sc = s_ref[g, :, lo:hi]
                e = jnp.where(sc > 0.5 * NEG_BIG, jnp.exp(sc - row_max[g]), 0.0)
                part = l_ref[g]
                for k in range(SEL_CHUNK // LANE):
                    part = part + e[:, k * LANE:(k + 1) * LANE]
                l_ref[g] = part
                acc_ref[g] = acc_ref[g] + _dot_nt(e.astype(BF16), vv)

    outs = [jnp.zeros((Q_TILE, MXU_DIM), F32) for _ in range(R_A)]
    for g in range(G_A):
        q3, slope3 = q3s[g], slopes[g]
        o_s = acc_ref[g] / jnp.maximum(jnp.sum(l_ref[g], axis=-1, keepdims=True), 1e-30)

        ss, ms = [], []
        for c in range(N_WIN_BLOCKS):
            tk = start - WINDOW + Q_TILE * c + lax.broadcasted_iota(jnp.int32, (1, Q_TILE), 1)
            dist = tq3 - tk
            ss.append(_dot(q3, win_refs[c][0:D_KV_A, :].astype(BF16)) - slope3 * dist.astype(F32))
            ms.append((dist >= 0) & (dist < WINDOW) & (tk >= 0))
        p_w = _masked_softmax(jnp.concatenate(ss, axis=-1), jnp.concatenate(ms, axis=-1))
        o_w = jnp.zeros((rows3, MXU_DIM), F32)
        for c in range(N_WIN_BLOCKS):
            o_w = o_w + _dot_nt(p_w[:, c * Q_TILE:(c + 1) * Q_TILE].astype(BF16),
                                win_refs[c][D_KV_A:2 * D_KV_A, :].astype(BF16))

        in_group = (lane_out >= g * HEAD_DIM) & (lane_out < (g + 1) * HEAD_DIM)
        for r in range(R_A):
            col = 3 * (R_A * g + r)
            rs = slice(r * Q_TILE, (r + 1) * Q_TILE)
            merged = (gate_ref[:, col:col + 1] * o_cs[g][rs] + gate_ref[:, col + 1:col + 2] * o_s[rs]
                      + gate_ref[:, col + 2:col + 3] * o_w[rs])
            outs[r] = jnp.where(in_group, merged, outs[r])

    for r in range(R_A):
        out_ref[:, r * MXU_DIM:(r + 1) * MXU_DIM] = outs[r]


def _nsa_prompt(qa, gates, kc, vc, nsa_t, win_t, place, expand, b, t):
    nq = t // Q_TILE
    cmp_spec = pl.BlockSpec((None, kc.shape[1], D_KV_A), lambda i, j: (i, 0, 0))
    rows3 = R_A * Q_TILE

    def win_spec(c):
        return pl.BlockSpec((None, 2 * D_KV_A, Q_TILE),
                            lambda i, j: (i, 0, jnp.maximum(j - (N_WIN_BLOCKS - 1) + c, 0)))

    return pl.pallas_call(
        _nsa_prompt_kernel,
        grid=(b, nq),
        in_specs=[pl.BlockSpec((Q_TILE, D_A), lambda i, j: (i * nq + j, 0)),
                  pl.BlockSpec((Q_TILE, LANE), lambda i, j: (i * nq + j, 0)),
                  cmp_spec, cmp_spec,
                  pl.BlockSpec((None, 2 * D_KV_A, t), lambda i, j: (i, 1, 0))]
        + [win_spec(c) for c in range(N_WIN_BLOCKS)]
        + [_const_spec(place.shape), _const_spec(expand.shape)],
        out_specs=pl.BlockSpec((Q_TILE, D_A), lambda i, j: (i * nq + j, 0)),
        out_shape=jax.ShapeDtypeStruct((b * t, D_A), F32),
        scratch_shapes=[pltpu.VMEM((G_A * Q_TILE, t), F32),
                        pltpu.VMEM((G_A, rows3, t), F32),
                        pltpu.VMEM((G_A, rows3, LANE), F32), pltpu.VMEM((G_A, rows3, LANE), F32),
                        pltpu.VMEM((G_A, rows3, MXU_DIM), F32)],
        compiler_params=_cparams(("arbitrary", "arbitrary")),
        name="nsa_prompt",
    )(qa, gates, kc, vc, nsa_t, *([win_t] * N_WIN_BLOCKS), place, expand)


SB_TILE = 256
SB_Q_TILE = 256


def _sb_prompt_kernel(q_ref, kt_ref, vt_ref, u_ref, out_ref, carry_ref, acc_ref):
    qi = pl.program_id(2)
    n_chunk_total = kt_ref.shape[1] // SB_TILE
    lane = lax.broadcasted_iota(jnp.int32, (1, LANE), 1)
    q = q_ref[...] * QK_SCALE
    qh = [jnp.where(lane < HEAD_DIM, q, 0.0).astype(BF16), jnp.where(lane >= HEAD_DIM, q, 0.0).astype(BF16)]
    tq = qi * SB_Q_TILE + lax.broadcasted_iota(jnp.int32, (SB_Q_TILE, 1), 0)
    carry_ref[...] = jnp.zeros(carry_ref.shape, F32)
    acc_ref[...] = jnp.zeros(acc_ref.shape, F32)

    def chunks(js):
        u = u_ref[...]
        stage = []
        for j in js:
            lo, hi = j * SB_TILE, (j + 1) * SB_TILE
            kk = kt_ref[:, lo:hi].astype(BF16)
            vv = vt_ref[:, lo:hi].astype(BF16)
            tk = lo + lax.broadcasted_iota(jnp.int32, (1, SB_TILE), 1)
            valid = tk < tq
            for h in range(2):
                z = _dot(qh[h], kk)
                lf_all = _log_sigmoid_neg(z)
                lf = jnp.where(valid, lf_all, 0.0)
                stage.append((h, valid, vv, lf_all + z + _split_dot(lf, u), jnp.sum(lf, axis=-1, keepdims=True)))
        carry = [carry_ref[0], carry_ref[1]]
        acc = [acc_ref[0], acc_ref[1]]
        for h, valid, vv, logit, row_sum in stage:
            a = jnp.where(valid, jnp.exp(logit + carry[h]), 0.0)
            acc[h] = acc[h] + _dot_nt(a.astype(BF16), vv)
            carry[h] = carry[h] + row_sum
        for h in range(2):
            carry_ref[h] = carry[h]
            acc_ref[h] = acc[h]

    assert n_chunk_total % 2 == 0 and SB_Q_TILE == SB_TILE
    for p in reversed(range(n_chunk_total // 2)):
        @pl.when(2 * p + 1 <= qi)
        def _(p=p):
            chunks((2 * p + 1, 2 * p))

        @pl.when(2 * p == qi)
        def _(p=p):
            chunks((2 * p,))

    out_ref[...] = jnp.where(lane < HEAD_DIM, acc_ref[0], acc_ref[1])


def _sb_prompt(qb, sb_t, umat, b, t):
    nq = t // SB_Q_TILE
    n_pair = D_B // LANE
    return pl.pallas_call(
        _sb_prompt_kernel,
        grid=(b, n_pair, nq),
        in_specs=[pl.BlockSpec((SB_Q_TILE, LANE), lambda i, h, j: (i * nq + j, h)),
                  pl.BlockSpec((None, LANE, t), lambda i, h, j: (i, h, 0)),
                  pl.BlockSpec((None, LANE, t), lambda i, h, j: (i, n_pair + h, 0)),
                  _const_spec((SB_TILE, SB_TILE))],
        out_specs=pl.BlockSpec((SB_Q_TILE, LANE), lambda i, h, j: (i * nq + j, h)),
        out_shape=jax.ShapeDtypeStruct((b * t, D_B), F32),
        scratch_shapes=[pltpu.VMEM((2, SB_Q_TILE, 1), F32), pltpu.VMEM((2, SB_Q_TILE, LANE), F32)],
        compiler_params=_cparams(("arbitrary", "arbitrary", "arbitrary")),
        name="sb_prompt",
    )(qb, sb_t, sb_t, umat)


CONV_TILE = 256
CONV_HALO = 32


def _conv_finish(y, cb_ref, lg_ref, lb_ref):
    y = y + cb_ref[...]
    mu = jnp.mean(y, axis=-1, keepdims=True)
    d = y - mu
    var = jnp.mean(d * d, axis=-1, keepdims=True)
    z = d * lax.rsqrt(var + EPS) * lg_ref[...] + lb_ref[...]
    return z * _sigmoid(z)


def _conv_prompt_kernel(cin_ref, halo_ref, cw_ref, cb_ref, lg_ref, lb_ref, out_ref, state_ref, ext_ref):
    ti = pl.program_id(1)
    c = cin_ref[...]
    u = c[:, 0:D_C] * _sigmoid(c[:, D_C:])
    hc = halo_ref[...]
    uh = hc[:, 0:D_C] * _sigmoid(hc[:, D_C:])
    ext_ref[0:CONV_HALO, :] = jnp.where(ti == 0, 0.0, uh)
    ext_ref[CONV_HALO:, :] = u
    off = CONV_HALO - (CONV_W - 1)
    y = jnp.zeros((CONV_TILE, D_C), F32)
    for w in range(CONV_W):
        y = y + ext_ref[off + w:off + w + CONV_TILE, :] * cw_ref[w:w + 1, :]
    out_ref[...] = _conv_finish(y, cb_ref, lg_ref, lb_ref)
    state_ref[...] = ext_ref[CONV_HALO + CONV_TILE - (CONV_W - 1):, :]


def _conv_prompt(cin, cw, cb, lg, lb, b, t):
    nt = t // CONV_TILE
    per = CONV_TILE // CONV_HALO
    return pl.pallas_call(
        _conv_prompt_kernel,
        grid=(b, nt),
        in_specs=[pl.BlockSpec((CONV_TILE, 2 * D_C), lambda i, j: (i * nt + j, 0)),
                  pl.BlockSpec((CONV_HALO, 2 * D_C), lambda i, j: (jnp.maximum((i * nt + j) * per - 1, 0), 0)),
                  _const_spec((CONV_W, D_C)), _const_spec((1, D_C)), _const_spec((1, D_C)), _const_spec((1, D_C))],
        out_specs=[pl.BlockSpec((CONV_TILE, D_C), lambda i, j: (i * nt + j, 0)),
                   pl.BlockSpec((None, CONV_W - 1, D_C), lambda i, j: (i, 0, 0))],
        out_shape=[jax.ShapeDtypeStruct((b * t, D_C), F32), jax.ShapeDtypeStruct((b, CONV_W - 1, D_C), F32)],
        scratch_shapes=[pltpu.VMEM((CONV_HALO + CONV_TILE, D_C), F32)],
        compiler_params=_cparams(("arbitrary", "arbitrary")),
        name="conv_prompt",
    )(cin, cin, cw, cb, lg, lb)


def _conv_sample_kernel(cin_ref, buf_ref, cw_ref, cb_ref, lg_ref, lb_ref, out_ref, state_ref):
    c = cin_ref[...]
    u = c[:, 0:D_C] * _sigmoid(c[:, D_C:])
    y = u * cw_ref[CONV_W - 1:CONV_W, :]
    for w in range(CONV_W - 1):
        y = y + buf_ref[w] * cw_ref[w:w + 1, :]
    out_ref[...] = _conv_finish(y, cb_ref, lg_ref, lb_ref)
    for w in range(CONV_W - 2):
        state_ref[w] = buf_ref[w + 1]
    state_ref[CONV_W - 2] = u


def _conv_sample(cin, buf_t, layer, cw, cb, lg, lb):
    n = cin.shape[0]
    state_spec = pl.BlockSpec((CONV_W - 1, n, D_C), lambda i: (0, 0, 0))
    return pl.pallas_call(
        _conv_sample_kernel,
        grid=(1,),
        in_specs=[_const_spec((n, 2 * D_C)),
                  pl.BlockSpec((None, CONV_W - 1, n, D_C), lambda i: (layer, 0, 0, 0)),
                  _const_spec((CONV_W, D_C)), _const_spec((1, D_C)), _const_spec((1, D_C)), _const_spec((1, D_C))],
        out_specs=[_const_spec((n, D_C)), state_spec],
        out_shape=[jax.ShapeDtypeStruct((n, D_C), F32), jax.ShapeDtypeStruct((CONV_W - 1, n, D_C), F32)],
        compiler_params=_cparams(("arbitrary",)),
        name="conv_sample",
    )(cin, buf_t, cw, cb, lg, lb)


ROWS_S = R_A * SUBLANE
NSA_PAGES_PER_STEP = 8


def _nsa_sample_kernel(pt_ref, *refs, past_len, page_size):
    npg = NSA_PAGES_PER_STEP
    page_refs = refs[:npg]
    (qa_ref, gate_ref, new_ref, winnew_ref, winnewt_ref, kc_ref, vc_ref, win_ref, e2_ref,
     out_ref, winout_ref, q_ref, sel_ref, m_ref, l_ref, acc_ref, oc_ref, ow_ref) = refs[npg:]
    b = pl.program_id(0)
    j = pl.program_id(1)
    n_steps = pl.num_programs(1)
    tq = past_len
    n_cmp = kc_ref.shape[0]
    n_sel = (past_len + 1 + SEL_BLOCK - 1) // SEL_BLOCK
    sel_w = sel_ref.shape[0] * LANE
    rowi = lax.broadcasted_iota(jnp.int32, (ROWS_S, 1), 0)
    g_of_row = rowi & (SUBLANE - 1)
    r_of_row = lax.shift_right_logical(rowi, 3)
    slope = jnp.zeros((ROWS_S, 1), F32)
    for g in range(G_A):
        for r in range(R_A):
            slope = jnp.where((g_of_row == g) & (r_of_row == r), float(SLOPES[g, r]), slope)
    lane256 = lax.broadcasted_iota(jnp.int32, (ROWS_S, MXU_DIM), 1)
    own_lanes = lax.shift_right_logical(lane256, 6) == g_of_row

    @pl.when(j == 0)
    def _():
        qrow = qa_ref[...]
        lane_w = lax.broadcasted_iota(jnp.int32, (ROWS_S, MXU_DIM), 1)
        qmat = jnp.zeros((ROWS_S, MXU_DIM), F32)
        for g in range(G_A):
            for r in range(R_A):
                h = R_A * g + r
                blk = qrow[:, (h // 2) * LANE:(h // 2 + 1) * LANE]
                shift = (g * HEAD_DIM - (h % 2) * HEAD_DIM) % MXU_DIM
                wide = jnp.concatenate([blk, jnp.zeros((1, LANE), F32)], axis=-1)
                placed = pltpu.roll(wide, shift, axis=1) if shift else wide
                keep = (rowi == r * SUBLANE + g) & (lane_w >= g * HEAD_DIM) & (lane_w < (g + 1) * HEAD_DIM)
                qmat = jnp.where(keep, placed, qmat)
        q_ref[...] = (qmat * QK_SCALE).astype(BF16)
        q = q_ref[...]

        s = _dot_nt(q, kc_ref[...].astype(BF16))
        ends = lax.broadcasted_iota(jnp.int32, (1, n_cmp), 1) * CMP_STRIDE + (CMP_LEN - 1)
        s = s - slope * (float(tq) - ends.astype(F32))
        p_c = _masked_softmax(s, ends <= tq)
        oc_ref[...] = _dot(p_c.astype(BF16), vc_ref[...].astype(BF16))
        imp = p_c[0:SUBLANE] + p_c[SUBLANE:2 * SUBLANE] + p_c[2 * SUBLANE:3 * SUBLANE]
        imp = jnp.concatenate([imp, jnp.zeros((SUBLANE, sel_w - n_cmp), F32)], axis=-1)
        tq_col = jnp.full((SUBLANE, 1), tq, jnp.int32)
        sel = _select_blocks(_block_scores(imp, tq_col, n_sel), min(TOP_N, n_sel))
        for i in range(sel_w // LANE):
            sel_ref[i] = sel[:, i * LANE:(i + 1) * LANE]

        wb = win_ref.shape[1]
        kw = win_ref[0:D_KV_A, :].astype(BF16)
        vw = win_ref[D_KV_A:, :].astype(BF16)
        tk = (past_len - wb) + lax.broadcasted_iota(jnp.int32, (1, wb), 1)
        dist = tq - tk
        sw = _dot(q, kw) - slope * dist.astype(F32)
        mw = (dist >= 0) & (dist < WINDOW) & (tk >= 0)
        knew = winnew_ref[:, 0:D_KV_A].astype(BF16).astype(F32)
        vnew = winnew_ref[:, D_KV_A:].astype(BF16).astype(F32)
        s_new = jnp.sum(q.astype(F32) * knew, axis=-1, keepdims=True)
        sw = jnp.where(mw, sw, -jnp.inf)
        mx = jnp.maximum(jnp.max(sw, axis=-1, keepdims=True), s_new)
        e = jnp.where(mw, jnp.exp(sw - mx), 0.0)
        e_new = jnp.exp(s_new - mx)
        den = jnp.maximum(jnp.sum(e, axis=-1, keepdims=True) + e_new, 1e-30)
        p_w = e / den
        p_new = (e_new / den).astype(BF16).astype(F32)
        ow_ref[...] = _dot_nt(p_w.astype(BF16), vw) + p_new * vnew
        seq_lane = lax.broadcasted_iota(jnp.int32, winnewt_ref.shape, 1)
        new_col = jnp.sum(jnp.where(seq_lane == b, winnewt_ref[...], 0.0), axis=-1, keepdims=True)
        pos_lane = lax.broadcasted_iota(jnp.int32, (1, wb), 1)
        shifted = pltpu.roll(win_ref[...], wb - 1, axis=1)
        winout_ref[...] = jnp.where(pos_lane == wb - 1, new_col, shifted)

        m_ref[...] = jnp.full((ROWS_S, 1), NEG_BIG, F32)
        l_ref[...] = jnp.zeros((ROWS_S, 1), F32)
        acc_ref[...] = jnp.zeros((ROWS_S, MXU_DIM), F32)

    q = q_ref[...]
    step_keys = npg * page_size
    sel_tile = sel_ref[lax.shift_right_logical(j, 1)].astype(BF16)
    kmask = _dot(sel_tile, e2_ref[j & 1])
    kks, vvs = [], []
    for c in range(npg // 2):
        a, bpage = page_refs[2 * c], page_refs[2 * c + 1]
        kks.append(jnp.concatenate([a[0:D_KV_A, :], bpage[0:D_KV_A, :]], axis=1).astype(BF16))
        vvs.append(jnp.concatenate([a[D_KV_A:, :], bpage[D_KV_A:, :]], axis=1).astype(BF16))
    sc = jnp.concatenate([_dot(q, kk) for kk in kks], axis=1)
    pos = j * step_keys + lax.broadcasted_iota(jnp.int32, (1, step_keys), 1)
    valid = (jnp.concatenate([kmask] * R_A, axis=0) > 0.5) & (pos <= tq)
    sc = jnp.where(valid, sc - slope * (float(tq) - pos.astype(F32)), NEG_BIG)
    m_old = m_ref[...]
    m_new = jnp.maximum(m_old, jnp.max(sc, axis=-1, keepdims=True))
    e = jnp.where(valid, jnp.exp(sc - m_new), 0.0)
    alpha = jnp.exp(m_old - m_new)
    l_ref[...] = alpha * l_ref[...] + jnp.sum(e, axis=-1, keepdims=True)
    pv = jnp.zeros((ROWS_S, MXU_DIM), F32)
    chunk = 2 * page_size
    for c in range(npg // 2):
        pv = pv + _dot_nt(e[:, c * chunk:(c + 1) * chunk].astype(BF16), vvs[c])
    acc_ref[...] = alpha * acc_ref[...] + pv
    m_ref[...] = m_new

    @pl.when(j == n_steps - 1)
    def _():
        knew = new_ref[:, 2 * D_KV_A:3 * D_KV_A].astype(BF16).astype(F32)
        vnew = new_ref[:, 3 * D_KV_A:].astype(BF16).astype(F32)
        lane_new = 4 * (tq // SEL_BLOCK)
        sel_new = sel_ref[lane_new // LANE][:, lane_new % LANE:lane_new % LANE + 1]
        ok = jnp.concatenate([sel_new] * R_A, axis=0) > 0.5
        s_new = jnp.where(ok, jnp.sum(q.astype(F32) * knew, axis=-1, keepdims=True), NEG_BIG)
        m_old = m_ref[...]
        m_new = jnp.maximum(m_old, s_new)
        e_new = jnp.where(ok, jnp.exp(s_new - m_new), 0.0)
        alpha = jnp.exp(m_old - m_new)
        l_fin = alpha * l_ref[...] + e_new
        acc = alpha * acc_ref[...] + e_new.astype(BF16).astype(F32) * vnew
        o_s = acc / jnp.maximum(l_fin, 1e-30)

        grow = gate_ref[...]
        lane_g = lax.broadcasted_iota(jnp.int32, (ROWS_S, LANE), 1)
        col0 = 3 * (R_A * g_of_row + r_of_row)
        gsel = lambda c: jnp.sum(jnp.where(lane_g == col0 + c, grow, 0.0), axis=-1, keepdims=True)
        merged = gsel(0) * oc_ref[...] + gsel(1) * o_s + gsel(2) * ow_ref[...]
        merged = jnp.where(own_lanes & (g_of_row < G_A), merged, 0.0)
        for r in range(R_A):
            out_ref[:, r * MXU_DIM:(r + 1) * MXU_DIM] = jnp.sum(
                merged[r * SUBLANE:(r + 1) * SUBLANE], axis=0, keepdims=True)


def _nsa_sample(page_table, cache_t, layer, qa, gates, nsa_new, win_new, win_new_t, kc, vc, win_t, past_len):
    n_seq, n_pages = page_table.shape
    page_size = cache_t.shape[-1]
    npg = NSA_PAGES_PER_STEP
    n_steps = n_pages // npg
    wb = win_t.shape[-1]
    n_sel = (past_len + 1 + SEL_BLOCK - 1) // SEL_BLOCK
    sel_w = -(-(4 * n_sel) // LANE) * LANE
    tok = lambda w: pl.BlockSpec((None, 1, w), lambda b, j, pt: (b, 0, 0))

    def page_spec(k):
        return pl.BlockSpec((None, None, 2 * D_KV_A, page_size), lambda b, j, pt: (layer, pt[b, j * npg + k], 1, 0))

    cmp_spec = pl.BlockSpec((None, kc.shape[1], D_KV_A), lambda b, j, pt: (b, 0, 0))
    win_in_spec = pl.BlockSpec((None, None, 2 * D_KV_A, wb), lambda b, j, pt: (layer, b, 0, 0))
    win_out_spec = pl.BlockSpec((None, 2 * D_KV_A, wb), lambda b, j, pt: (b, 0, 0))
    newt_spec = pl.BlockSpec((2 * D_KV_A, n_seq), lambda b, j, pt: (0, 0))
    step_keys = npg * page_size
    lanes_per_step = 4 * step_keys // SEL_BLOCK
    assert 2 * lanes_per_step == LANE
    e2_np = np.zeros((2, LANE, step_keys), np.float32)
    for v in range(2):
        for k in range(step_keys):
            e2_np[v, v * lanes_per_step + 4 * (k // SEL_BLOCK), k] = 1.0
    e2 = jnp.asarray(e2_np, BF16)
    e2_spec = pl.BlockSpec(e2.shape, lambda b, j, pt: (0, 0, 0))
    grid_spec = pltpu.PrefetchScalarGridSpec(
        num_scalar_prefetch=1,
        grid=(n_seq, n_steps),
        in_specs=[page_spec(k) for k in range(npg)]
        + [tok(D_A), tok(LANE), tok(4 * D_KV_A), tok(2 * D_KV_A), newt_spec, cmp_spec, cmp_spec, win_in_spec, e2_spec],
        out_specs=[tok(D_A), win_out_spec],
        scratch_shapes=[pltpu.VMEM((ROWS_S, MXU_DIM), BF16), pltpu.VMEM((sel_w // LANE, SUBLANE, LANE), F32),
                        pltpu.VMEM((ROWS_S, 1), F32), pltpu.VMEM((ROWS_S, 1), F32),
                        pltpu.VMEM((ROWS_S, MXU_DIM), F32), pltpu.VMEM((ROWS_S, MXU_DIM), F32),
                        pltpu.VMEM((ROWS_S, MXU_DIM), F32)],
    )
    r3 = lambda a: a.reshape(n_seq, 1, a.shape[-1])
    oa, win_out = pl.pallas_call(
        functools.partial(_nsa_sample_kernel, past_len=past_len, page_size=page_size),
        grid_spec=grid_spec,
        out_shape=[jax.ShapeDtypeStruct((n_seq, 1, D_A), F32), jax.ShapeDtypeStruct((n_seq, 2 * D_KV_A, wb), F32)],
        compiler_params=_cparams(("arbitrary", "arbitrary")),
        name="nsa_sample",
    )(page_table, *([cache_t] * npg), r3(qa), r3(gates), r3(nsa_new), r3(win_new), win_new_t, kc, vc, win_t, e2)
    return oa.reshape(n_seq, D_A), win_out


ROWS_B = 16
SB_PAGES_PER_STEP = 8


def _sb_sample_kernel(pt_ref, *refs, past_len, page_size, n_pages):
    npg = SB_PAGES_PER_STEP
    page_refs = refs[:npg]
    q_ref, new_ref, u_ref, out_ref, qm_ref, carry_ref, acc_ref = refs[npg:]
    j = pl.program_id(1)
    n_steps = pl.num_programs(1)
    tq = past_len
    rowi = lax.broadcasted_iota(jnp.int32, (ROWS_B, D_B), 0)
    lane = lax.broadcasted_iota(jnp.int32, (ROWS_B, D_B), 1)
    own = lax.shift_right_logical(lane, 6) == rowi

    @pl.when(j == 0)
    def _():
        qm = jnp.where(own, q_ref[...] * QK_SCALE, 0.0)
        qm_ref[...] = qm.astype(BF16)
        acc_ref[...] = jnp.zeros(acc_ref.shape, F32)
        knew = new_ref[:, 0:D_B].astype(BF16).astype(F32)
        z = jnp.sum(qm.astype(BF16).astype(F32) * knew, axis=-1, keepdims=True)
        valid = jnp.full((ROWS_B, 1), past_len < tq)
        lf = jnp.where(valid, _log_sigmoid_neg(z), 0.0)
        a = jnp.where(valid, jnp.exp(_log_sigmoid_neg(z) + z), 0.0)
        acc_ref[...] = a.astype(BF16).astype(F32) * new_ref[:, D_B:].astype(BF16).astype(F32)
        carry_ref[...] = lf

    qm = qm_ref[...]
    u = u_ref[...]
    for c in range(npg // 2):
        newer, older = page_refs[2 * c], page_refs[2 * c + 1]
        pg_old = (n_pages - 1) - (j * npg + 2 * c + 1)
        kk = jnp.concatenate([older[0:D_B, :], newer[0:D_B, :]], axis=1).astype(BF16)
        vv = jnp.concatenate([older[D_B:, :], newer[D_B:, :]], axis=1).astype(BF16)
        tk = pg_old * page_size + lax.broadcasted_iota(jnp.int32, (1, 2 * page_size), 1)
        valid = tk < tq
        z = _dot(qm, kk)
        lf_all = _log_sigmoid_neg(z)
        lf = jnp.where(valid, lf_all, 0.0)
        later = carry_ref[...] + _split_dot(lf, u)
        a = jnp.where(valid, jnp.exp(lf_all + z + later), 0.0)
        acc_ref[...] = acc_ref[...] + _dot_nt(a.astype(BF16), vv)
        carry_ref[...] = carry_ref[...] + jnp.sum(lf, axis=-1, keepdims=True)

    @pl.when(j == n_steps - 1)
    def _():
        out_ref[...] = jnp.sum(jnp.where(own, acc_ref[...], 0.0), axis=0, keepdims=True)


def _sb_sample(page_table, cache_t, layer, qb, sb_new, umat, past_len):
    n_seq, n_pages = page_table.shape
    page_size = cache_t.shape[-1]
    npg = SB_PAGES_PER_STEP
    n_steps = n_pages // npg
    tok = lambda w: pl.BlockSpec((None, 1, w), lambda b, j, pt: (b, 0, 0))

    def page_spec(k):
        return pl.BlockSpec((None, None, 2 * D_B, page_size),
                            lambda b, j, pt: (layer, pt[b, n_pages - 1 - (j * npg + k)], 0, 0))

    grid_spec = pltpu.PrefetchScalarGridSpec(
        num_scalar_prefetch=1,
        grid=(n_seq, n_steps),
        in_specs=[page_spec(k) for k in range(npg)]
        + [tok(D_B), tok(2 * D_B), pl.BlockSpec((2 * page_size, 2 * page_size), lambda b, j, pt: (0, 0))],
        out_specs=tok(D_B),
        scratch_shapes=[pltpu.VMEM((ROWS_B, D_B), BF16), pltpu.VMEM((ROWS_B, 1), F32), pltpu.VMEM((ROWS_B, D_B), F32)],
    )
    r3 = lambda a: a.reshape(n_seq, 1, a.shape[-1])
    out = pl.pallas_call(
        functools.partial(_sb_sample_kernel, past_len=past_len, page_size=page_size, n_pages=n_pages),
        grid_spec=grid_spec,
        out_shape=jax.ShapeDtypeStruct((n_seq, 1, D_B), F32),
        compiler_params=_cparams(("arbitrary", "arbitrary")),
        name="sb_sample",
    )(page_table, *([cache_t] * npg), r3(qb), r3(sb_new), umat)
    return out.reshape(n_seq, D_B)


def _oa_perm():
    perm = np.zeros((D_A,), np.int32)
    for r in range(R_A):
        for g in range(G_A):
            for d in range(HEAD_DIM):
                perm[r * MXU_DIM + g * HEAD_DIM + d] = g * R_A * HEAD_DIM + r * HEAD_DIM + d
    return perm


def _place_mats():
    place = np.zeros((H_A, LANE, MXU_DIM), np.float32)
    for g in range(G_A):
        for r in range(R_A):
            h = R_A * g + r
            for d in range(HEAD_DIM):
                place[h, (h % 2) * HEAD_DIM + d, g * HEAD_DIM + d] = 1.0
    return place


def _expand_mat(t):
    e = np.zeros((Q_TILE, t), np.float32)
    for k in range(t):
        e[4 * (k // SEL_BLOCK), k] = 1.0
    return e


def _prep_layer(l, w_in, w_out, norm_mix, norm_ffn, q_norm, k_norm, cmp_pos, cmp_w1, cmp_w2,
                out_norm_a, out_norm_b, conv_w, conv_b, conv_ln_g, conv_ln_b, w_gate, w_up, w_down):
    wi = w_in[l]
    w_tok = jnp.concatenate(
        [wi[:, :S_KV], wi[:, S_QB:S_KB], wi[:, S_CIN:], wi[:, S_GATE:S_QB], jnp.zeros((D_MODEL, LANE - N_GATE), F32),
         wi[:, S_KV:S_KV + 2 * D_KV_A]], axis=1).astype(BF16)
    wt = jnp.transpose(wi)
    w_feat = jnp.concatenate([wt[S_KV:S_GATE], wt[S_KB:S_CIN]], axis=0).astype(BF16)
    perm = _oa_perm()
    wo = w_out[l]
    w_out_p = jnp.concatenate([wo[:D_A][perm], wo[D_A:]], axis=0).astype(BF16)
    eye = jnp.eye(G_A, dtype=F32)
    w1 = cmp_w1[l].reshape(2, CMP_LEN, HEAD_DIM, HEAD_DIM)
    w1bd = jnp.einsum('gh,kpde->kpgdhe', eye, w1).reshape(2, CMP_LEN, D_KV_A, D_KV_A).astype(BF16)
    w2bd = jnp.einsum('gh,kde->kgdhe', eye, cmp_w2[l]).reshape(2, D_KV_A, D_KV_A).astype(BF16)
    tile4 = lambda v: jnp.tile(v, G_A).reshape(1, D_KV_A)
    col4 = lambda v: jnp.tile(v, G_A).reshape(D_KV_A, 1)
    return dict(
        w_tok=w_tok, w_feat=w_feat, w_out_p=w_out_p, w1bd=w1bd, w2bd=w2bd,
        pos4=jnp.tile(cmp_pos[l], (1, 1, G_A)),
        gn_mix=norm_mix[l].reshape(1, D_MODEL), gn_ffn=norm_ffn[l].reshape(1, D_MODEL),
        qn4=tile4(q_norm[l]), kn0_4=tile4(k_norm[l, 0]), kn1_4=tile4(k_norm[l, 1]), kn2_4=tile4(k_norm[l, 2]),
        kn1c=col4(k_norm[l, 1]), kn2c=col4(k_norm[l, 2]),
        ga=out_norm_a[l][perm].reshape(1, D_A), gb=out_norm_b[l].reshape(1, D_B),
        cw=conv_w[l], cb=conv_b[l].reshape(1, D_C), lg=conv_ln_g[l].reshape(1, D_C), lb=conv_ln_b[l].reshape(1, D_C),
        wg=w_gate[l].astype(BF16), wu=w_up[l].astype(BF16), wd=w_down[l].astype(BF16),
    )


def kernel(x_prompt, x_sample, cache_nsa, cache_sb, state_win, state_conv, page_table, w_in, w_out, norm_mix,
           norm_ffn, q_norm, k_norm, cmp_pos, cmp_w1, cmp_w2, out_norm_a, out_norm_b, conv_w, conv_b, conv_ln_g,
           conv_ln_b, w_gate, w_up, w_down):
    b, t, _ = x_prompt.shape
    n_seq, dec_seq, _ = x_sample.shape
    assert dec_seq == 1 and t % SB_TILE == 0 and t % (CMP_STRIDE * SUBLANE) == 0 and t >= WINDOW
    n_pool, page_size = cache_nsa.shape[1], cache_nsa.shape[2]
    n_pages = page_table.shape[1]
    past_len = n_pages * page_size
    assert page_size % CMP_STRIDE == 0 and page_size // CMP_STRIDE == SUBLANE
    assert n_pages % CMP_PAGES_PER_STEP == 0 and n_pages % NSA_PAGES_PER_STEP == 0 and n_pages % SB_PAGES_PER_STEP == 0

    pmat = jnp.asarray(np.kron(np.eye(G_A, dtype=np.float32), np.full((HEAD_DIM, HEAD_DIM), 1.0 / HEAD_DIM, np.float32)), BF16)
    place = jnp.asarray(_place_mats(), BF16)
    expand = jnp.asarray(_expand_mat(t), BF16)
    u_prompt = jnp.asarray(np.tril(np.ones((SB_TILE, SB_TILE), np.float32), -1), BF16)
    u_page = jnp.asarray(np.tril(np.ones((2 * page_size, 2 * page_size), np.float32), -1), BF16)

    fmaj = lambda a: jnp.transpose(a, (0, 1, 3, 4, 5, 2))
    cache_nsa_t = fmaj(cache_nsa).reshape(DEPTH, n_pool, 4 * D_KV_A, page_size)
    cache_sb_t = fmaj(cache_sb).reshape(DEPTH, n_pool, 2 * D_B, page_size)
    wb = state_win.shape[2]
    win_t = fmaj(state_win).reshape(DEPTH, n_seq, 2 * D_KV_A, wb)
    conv_t = jnp.transpose(state_conv, (0, 2, 1, 3))
    tmaj = lambda a, lead: jnp.transpose(a.reshape(a.shape[0], *lead, a.shape[-1]), (0, len(lead) + 1, *range(1, len(lead) + 1)))

    xp = x_prompt.reshape(b * t, D_MODEL)
    xs = x_sample.reshape(n_seq, D_MODEL)
    outs = {k: [] for k in ("p_nsa", "p_sb", "p_win", "p_conv", "s_nsa", "s_sb", "s_win", "s_conv")}
    for l in range(DEPTH):
        w = _prep_layer(l, w_in, w_out, norm_mix, norm_ffn, q_norm, k_norm, cmp_pos, cmp_w1, cmp_w2, out_norm_a,
                        out_norm_b, conv_w, conv_b, conv_ln_g, conv_ln_b, w_gate, w_up, w_down)

        qa, nsa_t, winp_t, qb, sb_t, cin, gates, cmp_rows = _in_proj_prompt(
            xp, w["gn_mix"], w["w_tok"], w["w_feat"], w["qn4"], w["kn1c"], w["kn2c"], pmat, b, t)
        kc, vc = _compress_prompt(cmp_rows.reshape(b, t // CMP_STRIDE, CMP_STRIDE * 2 * D_KV_A),
                                  w["pos4"], w["w1bd"], w["w2bd"], w["kn0_4"], pmat)
        oa = _nsa_prompt(qa, gates, kc, vc, nsa_t, winp_t, place, expand, b, t)
        ob = _sb_prompt(qb, sb_t, u_prompt, b, t)
        cv, conv_state = _conv_prompt(cin, w["cw"], w["cb"], w["lg"], w["lb"], b, t)
        xp = _out_proj(oa, ob, cv, xp, w["ga"], w["gb"], w["w_out_p"], tm=512)
        xp = _ffn(xp, w["gn_ffn"], w["wg"], w["wu"], w["wd"], tm=1024, tf=512, tn=512)
        outs["p_nsa"].append(tmaj(nsa_t, (4, G_A, HEAD_DIM)))
        outs["p_sb"].append(tmaj(sb_t, (2, H_B, HEAD_DIM)))
        outs["p_win"].append(tmaj(winp_t[:, :, t - min(WINDOW, t):], (2, G_A, HEAD_DIM)))
        outs["p_conv"].append(conv_state)

        qa, nsa_new, win_new, qb, sb_new, cin, gates, win_new_t = _in_proj_sample(
            xs, w["gn_mix"], w["w_tok"], w["w_feat"], w["qn4"], w["kn1_4"], w["kn2_4"], w["kn2c"], pmat)
        kc, vc = _compress_sample(page_table, cache_nsa_t, l, w["pos4"], w["w1bd"], w["w2bd"], w["kn0_4"], pmat)
        oa, win_out_t = _nsa_sample(page_table, cache_nsa_t, l, qa, gates, nsa_new, win_new, win_new_t, kc, vc,
                                    win_t, past_len)
        ob = _sb_sample(page_table, cache_sb_t, l, qb, sb_new, u_page, past_len)
        cv, conv_state_t = _conv_sample(cin, conv_t, l, w["cw"], w["cb"], w["lg"], w["lb"])
        xs = _out_proj(oa, ob, cv, xs, w["ga"], w["gb"], w["w_out_p"], tm=n_seq)
        xs = _ffn(xs, w["gn_ffn"], w["wg"], w["wu"], w["wd"], tm=n_seq, tf=512, tn=512)
        outs["s_nsa"].append(nsa_new.reshape(n_seq, 1, 4, G_A, HEAD_DIM))
        outs["s_sb"].append(sb_new.reshape(n_seq, 1, 2, H_B, HEAD_DIM))
        outs["s_win"].append(tmaj(win_out_t, (2, G_A, HEAD_DIM)))
        outs["s_conv"].append(jnp.transpose(conv_state_t, (1, 0, 2)))

    st = lambda k: jnp.stack(outs[k])
    return (xp.reshape(b, t, D_MODEL), xs.reshape(n_seq, 1, D_MODEL), st("p_nsa"), st("p_sb"), st("p_win"),
            st("p_conv"), st("s_nsa"), st("s_sb"), st("s_win"), st("s_conv"))
```

```python
import functools

import numpy as np
import jax
import jax.numpy as jnp
from jax import lax
from jax.experimental import pallas as pl
from jax.experimental.pallas import tpu as pltpu

F32 = jnp.float32
BF16 = jnp.bfloat16

LANE = 128
SUBLANE = 8
MXU_DIM = 256
VMEM_LIMIT_BYTES = 56 * 1024 * 1024

D_MODEL = 2048
DEPTH = 2
HEAD_DIM = 64
H_A = 12
G_A = 4
R_A = H_A // G_A
H_B = 12
D_A = H_A * HEAD_DIM
D_KV_A = G_A * HEAD_DIM
D_B = H_B * HEAD_DIM
D_C = D_MODEL - D_A - D_B
CONV_W = 31
CMP_LEN = 32
CMP_STRIDE = 16
SEL_BLOCK = 64
TOP_N = 16
WINDOW = 512
FORCE_BONUS = 1000.0
N_GATE = 3 * H_A
D_IN = D_A + 6 * D_KV_A + N_GATE + 3 * D_B + 2 * D_C
D_FF = -(-8 * D_MODEL // (3 * 256)) * 256
EPS = 1e-6
QK_SCALE = HEAD_DIM ** -0.5
NEG_BIG = -1e30

S_KV = D_A
S_GATE = S_KV + 6 * D_KV_A
S_QB = S_GATE + N_GATE
S_KB = S_QB + D_B
S_CIN = S_KB + 2 * D_B
T_QA = 0
T_QB = T_QA + D_A
T_CIN = T_QB + D_B
T_GATE = T_CIN + 2 * D_C
T_CMP = T_GATE + LANE
W_TOK = T_CMP + 2 * D_KV_A
F_KV = 0
F_SB = F_KV + 6 * D_KV_A
W_FEAT = F_SB + 2 * D_B

SLOPES = (2.0 ** (-8.0 * np.arange(1, H_A + 1, dtype=np.float32) / H_A)).astype(np.float32).reshape(G_A, R_A)


def _cparams(sem):
    return pltpu.CompilerParams(dimension_semantics=sem, vmem_limit_bytes=VMEM_LIMIT_BYTES)


def _const_spec(shape):
    nd = len(shape)
    return pl.BlockSpec(shape, lambda *_: (0,) * nd)


def _dot(a, b):
    return jnp.dot(a, b, preferred_element_type=F32)


def _dot_nt(a, b):
    return lax.dot_general(a, b, (((1,), (1,)), ((), ())), preferred_element_type=F32)


def _split_dot(x, w_bf16):
    hi = x.astype(BF16)
    lo = (x - hi.astype(F32)).astype(BF16)
    return _dot(hi, w_bf16) + _dot(lo, w_bf16)


def _sigmoid(x):
    return 1.0 / (1.0 + jnp.exp(-x))


def _rmsnorm_rows(x, gain):
    ms = jnp.mean(x * x, axis=-1, keepdims=True)
    return x * lax.rsqrt(ms + EPS) * gain


def _head_rmsnorm(u, gain_tile, pmat):
    outs = []
    for c in range(u.shape[1] // MXU_DIM):
        uc = u[:, c * MXU_DIM:(c + 1) * MXU_DIM]
        ms = _split_dot(uc * uc, pmat)
        outs.append(uc * lax.rsqrt(ms + EPS) * gain_tile)
    return outs[0] if len(outs) == 1 else jnp.concatenate(outs, axis=-1)


def _masked_softmax(s, mask):
    s = jnp.where(mask, s, -jnp.inf)
    m = jnp.max(s, axis=-1, keepdims=True)
    m = jnp.where(m == -jnp.inf, 0.0, m)
    e = jnp.where(mask, jnp.exp(s - m), 0.0)
    return e / jnp.maximum(jnp.sum(e, axis=-1, keepdims=True), 1e-30)


def _log_sigmoid_neg(z):
    return -(jnp.maximum(z, 0.0) + jnp.log(1.0 + jnp.exp(-jnp.abs(z))))


def _gelu_tanh(x):
    return 0.5 * x * (1.0 + jnp.tanh(np.float32(np.sqrt(2.0 / np.pi)) * (x + 0.044715 * (x * x * x))))


def _head_rmsnorm_t(u, gain_col, pmat):
    sq = u * u
    hi = sq.astype(BF16)
    lo = (sq - hi.astype(F32)).astype(BF16)
    ms = _dot(pmat, hi) + _dot(pmat, lo)
    return u * lax.rsqrt(ms + EPS) * gain_col


def _in_proj_prompt_kernel(x_ref, gn_ref, wtok_ref, wfeat_ref, qn_ref, kn1c_ref, kn2c_ref, pmat_ref,
                           qa_ref, nsat_ref, wint_ref, qb_ref, sbt_ref, cin_ref, gate_ref, cmp_ref):
    xn = _rmsnorm_rows(x_ref[...], gn_ref[...]).astype(BF16)
    pmat = pmat_ref[...]
    tok = lambda lo, hi: _dot(xn, wtok_ref[:, lo:hi])
    feat = lambda lo, hi: _dot_nt(wfeat_ref[lo:hi, :], xn)

    qa_ref[...] = _head_rmsnorm(tok(T_QA, T_QB), qn_ref[...], pmat)
    kvt = feat(F_KV, F_SB)
    nsat_ref[0:2 * D_KV_A, :] = kvt[0:2 * D_KV_A]
    nsat_ref[2 * D_KV_A:3 * D_KV_A, :] = _head_rmsnorm_t(kvt[2 * D_KV_A:3 * D_KV_A], kn1c_ref[...], pmat)
    nsat_ref[3 * D_KV_A:4 * D_KV_A, :] = kvt[3 * D_KV_A:4 * D_KV_A]
    wint_ref[0:D_KV_A, :] = _head_rmsnorm_t(kvt[4 * D_KV_A:5 * D_KV_A], kn2c_ref[...], pmat)
    wint_ref[D_KV_A:2 * D_KV_A, :] = kvt[5 * D_KV_A:6 * D_KV_A]
    qb_ref[...] = tok(T_QB, T_CIN)
    sbt_ref[...] = feat(F_SB, W_FEAT)
    cin_ref[...] = tok(T_CIN, T_GATE)
    gate_ref[...] = _sigmoid(tok(T_GATE, T_CMP))
    cmp_ref[...] = tok(T_CMP, W_TOK)


IN_PROJ_TILE = 256


def _resident_spec(shape):
    return pl.BlockSpec(shape, lambda *_: (0,) * len(shape), pipeline_mode=pl.Buffered(1))


def _in_proj_prompt(x2d, gn, w_tok, w_feat, qn4, kn1c, kn2c, pmat, b, t):
    tm = IN_PROJ_TILE
    nt = t // tm
    row = lambda w: pl.BlockSpec((tm, w), lambda i: (i, 0))
    fmaj = lambda f: pl.BlockSpec((None, f, tm), lambda i: (i // nt, 0, i % nt))
    tok_out = lambda w: jax.ShapeDtypeStruct((b * t, w), F32)
    feat_out = lambda f: jax.ShapeDtypeStruct((b, f, t), F32)
    col = _const_spec((MXU_DIM, 1))
    return pl.pallas_call(
        _in_proj_prompt_kernel,
        grid=(b * nt,),
        in_specs=[row(D_MODEL), _const_spec((1, D_MODEL)),
                  _resident_spec((D_MODEL, W_TOK)), _resident_spec((W_FEAT, D_MODEL)),
                  _const_spec((1, MXU_DIM)), col, col, _const_spec((MXU_DIM, MXU_DIM))],
        out_specs=[row(D_A), fmaj(4 * D_KV_A), fmaj(2 * D_KV_A), row(D_B), fmaj(2 * D_B), row(2 * D_C), row(LANE),
                   row(2 * D_KV_A)],
        out_shape=[tok_out(D_A), feat_out(4 * D_KV_A), feat_out(2 * D_KV_A), tok_out(D_B), feat_out(2 * D_B),
                   tok_out(2 * D_C), tok_out(LANE), tok_out(2 * D_KV_A)],
        compiler_params=_cparams(("arbitrary",)),
        name="in_proj_prompt",
    )(x2d, gn, w_tok, w_feat, qn4, kn1c, kn2c, pmat)


def _in_proj_sample_kernel(x_ref, gn_ref, wtok_ref, wfeat_ref, qn_ref, kn1_ref, kn2_ref, kn2c_ref, pmat_ref,
                           qa_ref, nsa_ref, win_ref, qb_ref, sb_ref, cin_ref, gate_ref, wint_ref):
    xn = _rmsnorm_rows(x_ref[...], gn_ref[...]).astype(BF16)
    pmat = pmat_ref[...]
    tok = lambda lo, hi: _dot(xn, wtok_ref[:, lo:hi])
    tok_f = lambda lo, hi: _dot_nt(xn, wfeat_ref[lo:hi, :])

    qa_ref[...] = _head_rmsnorm(tok(T_QA, T_QB), qn_ref[...], pmat)
    kv = tok_f(F_KV, F_SB)
    nsa_ref[:, 0:2 * D_KV_A] = kv[:, 0:2 * D_KV_A]
    nsa_ref[:, 2 * D_KV_A:3 * D_KV_A] = _head_rmsnorm(kv[:, 2 * D_KV_A:3 * D_KV_A], kn1_ref[...], pmat)
    nsa_ref[:, 3 * D_KV_A:4 * D_KV_A] = kv[:, 3 * D_KV_A:4 * D_KV_A]
    win_ref[:, 0:D_KV_A] = _head_rmsnorm(kv[:, 4 * D_KV_A:5 * D_KV_A], kn2_ref[...], pmat)
    win_ref[:, D_KV_A:2 * D_KV_A] = kv[:, 5 * D_KV_A:6 * D_KV_A]
    qb_ref[...] = tok(T_QB, T_CIN)
    sb_ref[...] = tok_f(F_SB, W_FEAT)
    cin_ref[...] = tok(T_CIN, T_GATE)
    gate_ref[...] = _sigmoid(tok(T_GATE, T_CMP))
    wt_win = _dot_nt(wfeat_ref[F_KV + 4 * D_KV_A:F_SB, :], xn)
    wint_ref[0:D_KV_A, :] = _head_rmsnorm_t(wt_win[0:D_KV_A], kn2c_ref[...], pmat)
    wint_ref[D_KV_A:, :] = wt_win[D_KV_A:]


def _in_proj_sample(x2d, gn, w_tok, w_feat, qn4, kn1_4, kn2_4, kn2c, pmat):
    m = x2d.shape[0]
    widths = (D_A, 4 * D_KV_A, 2 * D_KV_A, D_B, 2 * D_B, 2 * D_C, LANE)
    full = lambda r, c: _const_spec((r, c))
    return pl.pallas_call(
        _in_proj_sample_kernel,
        grid=(1,),
        in_specs=[full(m, D_MODEL), full(1, D_MODEL),
                  _resident_spec((D_MODEL, W_TOK)), _resident_spec((W_FEAT, D_MODEL)),
                  full(1, MXU_DIM), full(1, MXU_DIM), full(1, MXU_DIM), full(MXU_DIM, 1), full(MXU_DIM, MXU_DIM)],
        out_specs=[full(m, w) for w in widths] + [full(2 * D_KV_A, m)],
        out_shape=[jax.ShapeDtypeStruct((m, w), F32) for w in widths]
        + [jax.ShapeDtypeStruct((2 * D_KV_A, m), F32)],
        compiler_params=_cparams(("arbitrary",)),
        name="in_proj_sample",
    )(x2d, gn, w_tok, w_feat, qn4, kn1_4, kn2_4, kn2c, pmat)


def _out_proj_kernel(oa_ref, ob_ref, cv_ref, x_ref, ga_ref, gb_ref, w_ref, y_ref):
    ha = _rmsnorm_rows(oa_ref[...], ga_ref[...]).astype(BF16)
    hb = _rmsnorm_rows(ob_ref[...], gb_ref[...]).astype(BF16)
    hc = cv_ref[...].astype(BF16)
    y_ref[...] = x_ref[...] + _dot(jnp.concatenate([ha, hb, hc], axis=-1), w_ref[...])


def _out_proj(oa, ob, cv, x2d, ga, gb, w_out_p, tm):
    m = x2d.shape[0]
    row = lambda w: pl.BlockSpec((tm, w), lambda i: (i, 0))
    return pl.pallas_call(
        _out_proj_kernel,
        grid=(m // tm,),
        in_specs=[row(D_A), row(D_B), row(D_C), row(D_MODEL), _const_spec((1, D_A)), _const_spec((1, D_B)),
                  _const_spec((D_MODEL, D_MODEL))],
        out_specs=row(D_MODEL),
        out_shape=jax.ShapeDtypeStruct((m, D_MODEL), F32),
        compiler_params=_cparams(("arbitrary",)),
        name="out_proj",
    )(oa, ob, cv, x2d, ga, gb, w_out_p)


def _ffn_up_kernel(x_ref, gn_ref, wg_ref, wu_ref, h_ref, xn_ref):
    @pl.when(pl.program_id(1) == 0)
    def _():
        xn_ref[...] = _rmsnorm_rows(x_ref[...], gn_ref[...]).astype(BF16)

    xn = xn_ref[...]
    a = _dot(xn, wg_ref[...])
    b = _dot(xn, wu_ref[...])
    h_ref[...] = (a * _sigmoid(a) * b).astype(BF16)


def _ffn_down_kernel(h_ref, x_ref, wd_ref, y_ref):
    y_ref[...] = x_ref[...] + _dot(h_ref[...], wd_ref[...])


def _ffn(x2d, gn, wg, wu, wd, tm, tf, tn):
    m = x2d.shape[0]
    h = pl.pallas_call(
        _ffn_up_kernel,
        grid=(m // tm, D_FF // tf),
        in_specs=[pl.BlockSpec((tm, D_MODEL), lambda i, j: (i, 0)), _const_spec((1, D_MODEL)),
                  pl.BlockSpec((D_MODEL, tf), lambda i, j: (0, j)), pl.BlockSpec((D_MODEL, tf), lambda i, j: (0, j))],
        out_specs=pl.BlockSpec((tm, tf), lambda i, j: (i, j)),
        out_shape=jax.ShapeDtypeStruct((m, D_FF), BF16),
        scratch_shapes=[pltpu.VMEM((tm, D_MODEL), BF16)],
        compiler_params=_cparams(("arbitrary", "arbitrary")),
        name="ffn_up",
    )(x2d, gn, wg, wu)
    return pl.pallas_call(
        _ffn_down_kernel,
        grid=(m // tm, D_MODEL // tn),
        in_specs=[pl.BlockSpec((tm, D_FF), lambda i, j: (i, 0)), pl.BlockSpec((tm, tn), lambda i, j: (i, j)),
                  pl.BlockSpec((D_FF, tn), lambda i, j: (0, j))],
        out_specs=pl.BlockSpec((tm, tn), lambda i, j: (i, j)),
        out_shape=jax.ShapeDtypeStruct((m, D_MODEL), F32),
        compiler_params=_cparams(("arbitrary", "arbitrary")),
        name="ffn_down",
    )(h, x2d, wd)


def _compress_rows(get_piece, n_rows, pos_ref, w1_ref, w2_ref, kn0_ref, pmat_ref):
    outs = []
    for kv in range(2):
        pieces = [get_piece(p, kv) for p in range(CMP_STRIDE)]
        lhs = [jnp.concatenate([(pieces[p] + pos_ref[kv, half * CMP_STRIDE + p:half * CMP_STRIDE + p + 1, :]).astype(BF16)
                                for p in range(CMP_STRIDE)], axis=-1) for half in range(2)]
        first = _dot(lhs[0], w1_ref[kv, 0])
        second = _dot(lhs[1], w1_ref[kv, 1])
        pre = first + pltpu.roll(second, n_rows - 1, axis=0)
        o = _dot(_gelu_tanh(pre).astype(BF16), w2_ref[kv])
        if kv == 0:
            o = _head_rmsnorm(o, kn0_ref[...], pmat_ref[...])
        outs.append(o)
    return outs


def _compress_prompt_kernel(s0_ref, s1_ref, s2_ref, s3_ref, pos_ref, w1_ref, w2_ref, kn0_ref, pmat_ref,
                            kc_ref, vc_ref):
    slabs = (s0_ref, s1_ref, s2_ref, s3_ref)
    n_rows = s0_ref.shape[0] // CMP_STRIDE

    def piece(p, kv):
        return jnp.concatenate([slabs[2 * kv + h][pl.ds(p, n_rows, stride=CMP_STRIDE), :] for h in range(2)], axis=-1)

    kc, vc = _compress_rows(piece, n_rows, pos_ref, w1_ref, w2_ref, kn0_ref, pmat_ref)
    kc_ref[...] = kc
    vc_ref[...] = vc


def _compress_prompt(cmp_rows, pos4, w1bd, w2bd, kn0_4, pmat, b, t):
    nsub = t // CMP_STRIDE
    n_slab = 2 * D_KV_A // LANE
    return pl.pallas_call(
        _compress_prompt_kernel,
        grid=(b,),
        in_specs=[pl.BlockSpec((t, LANE), lambda i, s=s: (i, s)) for s in range(n_slab)]
        + [_const_spec(pos4.shape), _const_spec(w1bd.shape), _const_spec(w2bd.shape),
           _const_spec((1, MXU_DIM)), _const_spec((MXU_DIM, MXU_DIM))],
        out_specs=[pl.BlockSpec((None, nsub, D_KV_A), lambda i: (i, 0, 0))] * 2,
        out_shape=[jax.ShapeDtypeStruct((b, nsub, D_KV_A), F32)] * 2,
        compiler_params=_cparams(("arbitrary",)),
        name="compress_prompt",
    )(*([cmp_rows] * n_slab), pos4, w1bd, w2bd, kn0_4, pmat)


CMP_PAGES_PER_STEP = 32


def _compress_sample_kernel(pt_ref, *refs, page_size):
    npg = CMP_PAGES_PER_STEP
    page_refs = refs[:npg + 1]
    pos_ref, w1_ref, w2_ref, kn0_ref, pmat_ref, kc_ref, vc_ref, rows_ref = refs[npg + 1:]
    sub = page_size // CMP_STRIDE
    n_rows = (npg + 1) * sub
    n_slab = 2 * D_KV_A // LANE

    for k, pr in enumerate(page_refs):
        for s in range(n_slab):
            rows_ref[s, k * page_size:(k + 1) * page_size, :] = pr[s * LANE:(s + 1) * LANE, :].T

    def piece(p, kv):
        slabs = [rows_ref[2 * kv + h, pl.ds(p, n_rows, stride=CMP_STRIDE), :] for h in range(2)]
        return jnp.concatenate(slabs, axis=-1)

    kc, vc = _compress_rows(piece, n_rows, pos_ref, w1_ref, w2_ref, kn0_ref, pmat_ref)
    kc_ref[...] = kc[0:npg * sub]
    vc_ref[...] = vc[0:npg * sub]


def _compress_sample(page_table, cache_t, layer, pos4, w1bd, w2bd, kn0_4, pmat):
    n_seq, n_pages = page_table.shape
    page_size = cache_t.shape[-1]
    npg = CMP_PAGES_PER_STEP
    n_steps = n_pages // npg
    sub = page_size // CMP_STRIDE

    def page_spec(k):
        def imap(b, j, pt):
            return (layer, pt[b, jnp.minimum(j * npg + k, n_pages - 1)], 0, 0)
        return pl.BlockSpec((None, None, 2 * D_KV_A, page_size), imap)

    const = lambda shape: pl.BlockSpec(shape, lambda b, j, pt: (0,) * len(shape))
    out_spec = pl.BlockSpec((None, npg * sub, D_KV_A), lambda b, j, pt: (b, j, 0))
    grid_spec = pltpu.PrefetchScalarGridSpec(
        num_scalar_prefetch=1,
        grid=(n_seq, n_steps),
        in_specs=[page_spec(k) for k in range(npg + 1)]
        + [const(pos4.shape), const(w1bd.shape), const(w2bd.shape), const((1, MXU_DIM)), const((MXU_DIM, MXU_DIM))],
        out_specs=[out_spec, out_spec],
        scratch_shapes=[pltpu.VMEM((2 * D_KV_A // LANE, (npg + 1) * page_size, LANE), F32)],
    )
    return pl.pallas_call(
        functools.partial(_compress_sample_kernel, page_size=page_size),
        grid_spec=grid_spec,
        out_shape=[jax.ShapeDtypeStruct((n_seq, n_pages * sub, D_KV_A), F32)] * 2,
        compiler_params=_cparams(("arbitrary", "arbitrary")),
        name="compress_sample",
    )(page_table, *([cache_t] * (npg + 1)), pos4, w1bd, w2bd, kn0_4, pmat)


def _select_blocks(score, n_top):
    width = score.shape[1]
    per = SEL_BLOCK // CMP_STRIDE
    lane = lax.broadcasted_iota(jnp.int32, (1, width), 1)
    rank = jnp.zeros(score.shape, F32)
    for k in range(1, width // per):
        other = pltpu.roll(score, per * k, axis=1)
        from_lower = jnp.where(lane >= per * k, 1.0, 0.0)
        rank = rank + jnp.where(other > score, 1.0, jnp.where(other == score, from_lower, 0.0))
    return jnp.where(rank < n_top, jnp.where(score >= 0.0, 1.0, 0.0), 0.0)


def _block_scores(imp, tq_col, n_sel):
    width = imp.shape[1]
    per = SEL_BLOCK // CMP_STRIDE
    pooled = imp
    for k in range(1, per):
        pooled = pooled + pltpu.roll(imp, width - k, axis=1)
    lane = lax.broadcasted_iota(jnp.int32, imp.shape, 1)
    blk = lax.shift_right_logical(lane, 2)
    cur = lax.shift_right_logical(tq_col, 6)
    valid = blk <= cur
    forced = (blk == 0) | (blk == cur) | (blk == cur - 1)
    score = jnp.where(valid, jnp.where(forced, FORCE_BONUS, pooled), -1.0)
    cand = ((lane & (per - 1)) == 0) & (blk < n_sel)
    return jnp.where(cand, score, -jnp.inf)


Q_TILE = 128
SEL_CHUNK = 512
N_WIN_BLOCKS = WINDOW // Q_TILE + 1


def _nsa_prompt_kernel(qa_ref, gate_ref, kc_ref, vc_ref, ksv_ref, w0_ref, w1_ref, w2_ref, w3_ref, w4_ref,
                       place_ref, expand_ref, out_ref, s_ref, mx_ref, l_ref, acc_ref):
    win_refs = (w0_ref, w1_ref, w2_ref, w3_ref, w4_ref)
    t_len = ksv_ref.shape[1]
    n_cmp = kc_ref.shape[0]
    n_sel = t_len // SEL_BLOCK
    n_chunk_total = t_len // SEL_CHUNK
    qi = pl.program_id(1)
    start = qi * Q_TILE
    rows3 = R_A * Q_TILE
    rowi = lax.broadcasted_iota(jnp.int32, (rows3, 1), 0)
    r_of_row = lax.shift_right_logical(rowi, 7)
    tq3 = start + (rowi & (Q_TILE - 1))
    tq3f = tq3.astype(F32)
    tq1 = tq3[0:Q_TILE]
    lane_out = lax.broadcasted_iota(jnp.int32, (Q_TILE, MXU_DIM), 1)
    n_chunks = lax.shift_right_logical(start + Q_TILE + SEL_CHUNK - 1, 9)

    q3s, slopes, o_cs, scores = [], [], [], []
    kc = kc_ref[...].astype(BF16)
    vc = vc_ref[...].astype(BF16)
    ends = lax.broadcasted_iota(jnp.int32, (1, n_cmp), 1) * CMP_STRIDE + (CMP_LEN - 1)
    for g in range(G_A):
        qs = []
        for r in range(R_A):
            h = R_A * g + r
            qblk = qa_ref[:, (h // 2) * LANE:(h // 2 + 1) * LANE].astype(BF16)
            qs.append(_dot(qblk, place_ref[h]) * QK_SCALE)
        q3 = jnp.concatenate(qs, axis=0).astype(BF16)
        slope3 = jnp.where(r_of_row == 0, float(SLOPES[g, 0]),
                           jnp.where(r_of_row == 1, float(SLOPES[g, 1]), float(SLOPES[g, 2])))
        s = _dot_nt(q3, kc) - slope3 * (tq3f - ends.astype(F32))
        p_c = _masked_softmax(s, ends <= tq3)
        o_cs.append(_dot(p_c.astype(BF16), vc))
        imp = p_c[0:Q_TILE] + p_c[Q_TILE:2 * Q_TILE] + p_c[2 * Q_TILE:3 * Q_TILE]
        scores.append(_block_scores(imp, tq1, n_sel))
        q3s.append(q3)
        slopes.append(slope3)

    lane_q = lax.broadcasted_iota(jnp.int32, (1, n_cmp), 1)
    combined = scores[0]
    for g in range(1, G_A):
        combined = jnp.where((lane_q & (G_A - 1)) == g, pltpu.roll(scores[g], g, axis=1), combined)
    sel = _select_blocks(combined, min(TOP_N, n_sel)).astype(BF16)

    mx_ref[...] = jnp.full(mx_ref.shape, NEG_BIG, F32)
    l_ref[...] = jnp.zeros(l_ref.shape, F32)
    acc_ref[...] = jnp.zeros(acc_ref.shape, F32)
    for c in range(n_chunk_total):
        lo, hi = c * SEL_CHUNK, (c + 1) * SEL_CHUNK

        @pl.when(c < n_chunks)
        def _(lo=lo, hi=hi):
            kk = ksv_ref[0:D_KV_A, lo:hi].astype(BF16)
            pos = lo + lax.broadcasted_iota(jnp.int32, (1, SEL_CHUNK), 1)
            dist = tq3f - pos.astype(F32)
            causal = pos <= tq3
            for g in range(G_A):
                kmc = _dot(sel, expand_ref[g, :, lo:hi])
                valid = (jnp.concatenate([kmc, kmc, kmc], axis=0) > 0.5) & causal
                sc = jnp.where(valid, _dot(q3s[g], kk) - slopes[g] * dist, NEG_BIG)
                s_ref[g, :, lo:hi] = sc
                m = mx_ref[g]
                for k in range(SEL_CHUNK // LANE):
                    m = jnp.maximum(m, sc[:, k * LANE:(k + 1) * LANE])
                mx_ref[g] = m

    row_max = [jnp.max(mx_ref[g], axis=-1, keepdims=True) for g in range(G_A)]
    for c in range(n_chunk_total):
        lo, hi = c * SEL_CHUNK, (c + 1) * SEL_CHUNK

        @pl.when(c < n_chunks)
        def _(lo=lo, hi=hi):
            vv = ksv_ref[D_KV_A:2 * D_KV_A, lo:hi].astype(BF16)
            for g in range(G_A):
                sc = s_ref[g, :, lo:hi]
                e = jnp.where(sc > 0.5 * NEG_BIG, jnp.exp(sc - row_max[g]), 0.0)
                part = l_ref[g]
                for k in range(SEL_CHUNK // LANE):
                    part = part + e[:, k * LANE:(k + 1) * LANE]
                l_ref[g] = part
                acc_ref[g] = acc_ref[g] + _dot_nt(e.astype(BF16), vv)

    kw = jnp.concatenate([w[0:D_KV_A, :] for w in win_refs], axis=1).astype(BF16)
    vw = jnp.concatenate([w[D_KV_A:2 * D_KV_A, :] for w in win_refs], axis=1).astype(BF16)
    tk = start - WINDOW + lax.broadcasted_iota(jnp.int32, (1, N_WIN_BLOCKS * Q_TILE), 1)
    wdist = tq3 - tk
    wmask = (wdist >= 0) & (wdist < WINDOW) & (tk >= 0)
    wdistf = wdist.astype(F32)

    outs = [jnp.zeros((Q_TILE, MXU_DIM), F32) for _ in range(R_A)]
    for g in range(G_A):
        q3, slope3 = q3s[g], slopes[g]
        o_s = acc_ref[g] / jnp.maximum(jnp.sum(l_ref[g], axis=-1, keepdims=True), 1e-30)
        p_w = _masked_softmax(_dot(q3, kw) - slope3 * wdistf, wmask)
        o_w = _dot_nt(p_w.astype(BF16), vw)

        in_group = (lane_out >= g * HEAD_DIM) & (lane_out < (g + 1) * HEAD_DIM)
        for r in range(R_A):
            col = 3 * (R_A * g + r)
            rs = slice(r * Q_TILE, (r + 1) * Q_TILE)
            merged = (gate_ref[:, col:col + 1] * o_cs[g][rs] + gate_ref[:, col + 1:col + 2] * o_s[rs]
                      + gate_ref[:, col + 2:col + 3] * o_w[rs])
            outs[r] = jnp.where(in_group, merged, outs[r])

    for r in range(R_A):
        out_ref[:, r * MXU_DIM:(r + 1) * MXU_DIM] = outs[r]


def _nsa_prompt(qa, gates, kc, vc, nsa_t, win_t, place, expand, b, t):
    nq = t // Q_TILE
    cmp_spec = pl.BlockSpec((None, kc.shape[1], D_KV_A), lambda i, j: (i, 0, 0))
    rows3 = R_A * Q_TILE

    def win_spec(c):
        return pl.BlockSpec((None, 2 * D_KV_A, Q_TILE),
                            lambda i, j: (i, 0, jnp.maximum(j - (N_WIN_BLOCKS - 1) + c, 0)))

    return pl.pallas_call(
        _nsa_prompt_kernel,
        grid=(b, nq),
        in_specs=[pl.BlockSpec((Q_TILE, D_A), lambda i, j: (i * nq + j, 0)),
                  pl.BlockSpec((Q_TILE, LANE), lambda i, j: (i * nq + j, 0)),
                  cmp_spec, cmp_spec,
                  pl.BlockSpec((None, 2 * D_KV_A, t), lambda i, j: (i, 1, 0))]
        + [win_spec(c) for c in range(N_WIN_BLOCKS)]
        + [_const_spec(place.shape), _const_spec(expand.shape)],
        out_specs=pl.BlockSpec((Q_TILE, D_A), lambda i, j: (i * nq + j, 0)),
        out_shape=jax.ShapeDtypeStruct((b * t, D_A), F32),
        scratch_shapes=[pltpu.VMEM((G_A, rows3, t), F32),
                        pltpu.VMEM((G_A, rows3, LANE), F32), pltpu.VMEM((G_A, rows3, LANE), F32),
                        pltpu.VMEM((G_A, rows3, MXU_DIM), F32)],
        compiler_params=_cparams(("arbitrary", "arbitrary")),
        name="nsa_prompt",
    )(qa, gates, kc, vc, nsa_t, *([win_t] * N_WIN_BLOCKS), place, expand)


SB_TILE = 256
SB_Q_TILE = 256


def _sb_prompt_kernel(q_ref, kt_ref, vt_ref, u_ref, out_ref, carry_ref, acc_ref):
    qi = pl.program_id(2)
    n_chunk_total = kt_ref.shape[1] // SB_TILE
    lane = lax.broadcasted_iota(jnp.int32, (1, LANE), 1)
    q = q_ref[...] * QK_SCALE
    qh = [jnp.where(lane < HEAD_DIM, q, 0.0).astype(BF16), jnp.where(lane >= HEAD_DIM, q, 0.0).astype(BF16)]
    tq = qi * SB_Q_TILE + lax.broadcasted_iota(jnp.int32, (SB_Q_TILE, 1), 0)
    carry_ref[...] = jnp.zeros(carry_ref.shape, F32)
    acc_ref[...] = jnp.zeros(acc_ref.shape, F32)

    def chunks(js):
        u = u_ref[...]
        stage = []
        for j in js:
            lo, hi = j * SB_TILE, (j + 1) * SB_TILE
            kk = kt_ref[:, lo:hi].astype(BF16)
            vv = vt_ref[:, lo:hi].astype(BF16)
            tk = lo + lax.broadcasted_iota(jnp.int32, (1, SB_TILE), 1)
            valid = tk < tq
            for h in range(2):
                z = _dot(qh[h], kk)
                lf_all = _log_sigmoid_neg(z)
                lf = jnp.where(valid, lf_all, 0.0)
                stage.append((h, valid, vv, lf_all + z + _split_dot(lf, u), jnp.sum(lf, axis=-1, keepdims=True)))
        carry = [carry_ref[0], carry_ref[1]]
        acc = [acc_ref[0], acc_ref[1]]
        for h, valid, vv, logit, row_sum in stage:
            a = jnp.where(valid, jnp.exp(logit + carry[h]), 0.0)
            acc[h] = acc[h] + _dot_nt(a.astype(BF16), vv)
            carry[h] = carry[h] + row_sum
        for h in range(2):
            carry_ref[h] = carry[h]
            acc_ref[h] = acc[h]

    assert n_chunk_total % 2 == 0 and SB_Q_TILE == SB_TILE
    for p in reversed(range(n_chunk_total // 2)):
        @pl.when(2 * p + 1 <= qi)
        def _(p=p):
            chunks((2 * p + 1, 2 * p))

        @pl.when(2 * p == qi)
        def _(p=p):
            chunks((2 * p,))

    out_ref[...] = jnp.where(lane < HEAD_DIM, acc_ref[0], acc_ref[1])


def _sb_prompt(qb, sb_t, umat, b, t):
    nq = t // SB_Q_TILE
    n_pair = D_B // LANE
    return pl.pallas_call(
        _sb_prompt_kernel,
        grid=(b, n_pair, nq),
        in_specs=[pl.BlockSpec((SB_Q_TILE, LANE), lambda i, h, j: (i * nq + j, h)),
                  pl.BlockSpec((None, LANE, t), lambda i, h, j: (i, h, 0)),
                  pl.BlockSpec((None, LANE, t), lambda i, h, j: (i, n_pair + h, 0)),
                  _const_spec((SB_TILE, SB_TILE))],
        out_specs=pl.BlockSpec((SB_Q_TILE, LANE), lambda i, h, j: (i * nq + j, h)),
        out_shape=jax.ShapeDtypeStruct((b * t, D_B), F32),
        scratch_shapes=[pltpu.VMEM((2, SB_Q_TILE, 1), F32), pltpu.VMEM((2, SB_Q_TILE, LANE), F32)],
        compiler_params=_cparams(("arbitrary", "arbitrary", "arbitrary")),
        name="sb_prompt",
    )(qb, sb_t, sb_t, umat)


CONV_TILE = 256
CONV_HALO = 32


def _conv_finish(y, cb_ref, lg_ref, lb_ref):
    y = y + cb_ref[...]
    mu = jnp.mean(y, axis=-1, keepdims=True)
    d = y - mu
    var = jnp.mean(d * d, axis=-1, keepdims=True)
    z = d * lax.rsqrt(var + EPS) * lg_ref[...] + lb_ref[...]
    return z * _sigmoid(z)


def _conv_prompt_kernel(cin_ref, halo_ref, cw_ref, cb_ref, lg_ref, lb_ref, out_ref, state_ref, ext_ref):
    ti = pl.program_id(1)
    c = cin_ref[...]
    u = c[:, 0:D_C] * _sigmoid(c[:, D_C:])
    hc = halo_ref[...]
    uh = hc[:, 0:D_C] * _sigmoid(hc[:, D_C:])
    ext_ref[0:CONV_HALO, :] = jnp.where(ti == 0, 0.0, uh)
    ext_ref[CONV_HALO:, :] = u
    off = CONV_HALO - (CONV_W - 1)
    y = jnp.zeros((CONV_TILE, D_C), F32)
    for w in range(CONV_W):
        y = y + ext_ref[off + w:off + w + CONV_TILE, :] * cw_ref[w:w + 1, :]
    out_ref[...] = _conv_finish(y, cb_ref, lg_ref, lb_ref)
    state_ref[...] = ext_ref[CONV_HALO + CONV_TILE - (CONV_W - 1):, :]


def _conv_prompt(cin, cw, cb, lg, lb, b, t):
    nt = t // CONV_TILE
    per = CONV_TILE // CONV_HALO
    return pl.pallas_call(
        _conv_prompt_kernel,
        grid=(b, nt),
        in_specs=[pl.BlockSpec((CONV_TILE, 2 * D_C), lambda i, j: (i * nt + j, 0)),
                  pl.BlockSpec((CONV_HALO, 2 * D_C), lambda i, j: (jnp.maximum((i * nt + j) * per - 1, 0), 0)),
                  _const_spec((CONV_W, D_C)), _const_spec((1, D_C)), _const_spec((1, D_C)), _const_spec((1, D_C))],
        out_specs=[pl.BlockSpec((CONV_TILE, D_C), lambda i, j: (i * nt + j, 0)),
                   pl.BlockSpec((None, CONV_W - 1, D_C), lambda i, j: (i, 0, 0))],
        out_shape=[jax.ShapeDtypeStruct((b * t, D_C), F32), jax.ShapeDtypeStruct((b, CONV_W - 1, D_C), F32)],
        scratch_shapes=[pltpu.VMEM((CONV_HALO + CONV_TILE, D_C), F32)],
        compiler_params=_cparams(("arbitrary", "arbitrary")),
        name="conv_prompt",
    )(cin, cin, cw, cb, lg, lb)


def _conv_sample_kernel(cin_ref, buf_ref, cw_ref, cb_ref, lg_ref, lb_ref, out_ref, state_ref):
    c = cin_ref[...]
    u = c[:, 0:D_C] * _sigmoid(c[:, D_C:])
    y = u * cw_ref[CONV_W - 1:CONV_W, :]
    for w in range(CONV_W - 1):
        y = y + buf_ref[w] * cw_ref[w:w + 1, :]
    out_ref[...] = _conv_finish(y, cb_ref, lg_ref, lb_ref)
    for w in range(CONV_W - 2):
        state_ref[w] = buf_ref[w + 1]
    state_ref[CONV_W - 2] = u


def _conv_sample(cin, buf_t, layer, cw, cb, lg, lb):
    n = cin.shape[0]
    state_spec = pl.BlockSpec((CONV_W - 1, n, D_C), lambda i: (0, 0, 0))
    return pl.pallas_call(
        _conv_sample_kernel,
        grid=(1,),
        in_specs=[_const_spec((n, 2 * D_C)),
                  pl.BlockSpec((None, CONV_W - 1, n, D_C), lambda i: (layer, 0, 0, 0)),
                  _const_spec((CONV_W, D_C)), _const_spec((1, D_C)), _const_spec((1, D_C)), _const_spec((1, D_C))],
        out_specs=[_const_spec((n, D_C)), state_spec],
        out_shape=[jax.ShapeDtypeStruct((n, D_C), F32), jax.ShapeDtypeStruct((CONV_W - 1, n, D_C), F32)],
        compiler_params=_cparams(("arbitrary",)),
        name="conv_sample",
    )(cin, buf_t, cw, cb, lg, lb)


ROWS_S = R_A * SUBLANE
NSA_PAGES_PER_STEP = 8


def _nsa_sample_kernel(pt_ref, *refs, past_len, page_size):
    npg = NSA_PAGES_PER_STEP
    page_refs = refs[:npg]
    (qa_ref, gate_ref, new_ref, winnew_ref, winnewt_ref, kc_ref, vc_ref, win_ref, e2_ref,
     out_ref, winout_ref, q_ref, sel_ref, m_ref, l_ref, acc_ref, oc_ref, ow_ref) = refs[npg:]
    b = pl.program_id(0)
    j = pl.program_id(1)
    n_steps = pl.num_programs(1)
    tq = past_len
    n_cmp = kc_ref.shape[0]
    n_sel = (past_len + 1 + SEL_BLOCK - 1) // SEL_BLOCK
    sel_w = sel_ref.shape[0] * LANE
    rowi = lax.broadcasted_iota(jnp.int32, (ROWS_S, 1), 0)
    g_of_row = rowi & (SUBLANE - 1)
    r_of_row = lax.shift_right_logical(rowi, 3)
    slope = jnp.zeros((ROWS_S, 1), F32)
    for g in range(G_A):
        for r in range(R_A):
            slope = jnp.where((g_of_row == g) & (r_of_row == r), float(SLOPES[g, r]), slope)
    lane256 = lax.broadcasted_iota(jnp.int32, (ROWS_S, MXU_DIM), 1)
    own_lanes = lax.shift_right_logical(lane256, 6) == g_of_row

    @pl.when(j == 0)
    def _():
        qrow = qa_ref[...]
        lane_w = lax.broadcasted_iota(jnp.int32, (ROWS_S, MXU_DIM), 1)
        qmat = jnp.zeros((ROWS_S, MXU_DIM), F32)
        for g in range(G_A):
            for r in range(R_A):
                h = R_A * g + r
                blk = qrow[:, (h // 2) * LANE:(h // 2 + 1) * LANE]
                shift = (g * HEAD_DIM - (h % 2) * HEAD_DIM) % MXU_DIM
                wide = jnp.concatenate([blk, jnp.zeros((1, LANE), F32)], axis=-1)
                placed = pltpu.roll(wide, shift, axis=1) if shift else wide
                keep = (rowi == r * SUBLANE + g) & (lane_w >= g * HEAD_DIM) & (lane_w < (g + 1) * HEAD_DIM)
                qmat = jnp.where(keep, placed, qmat)
        q_ref[...] = (qmat * QK_SCALE).astype(BF16)
        q = q_ref[...]

        s = _dot_nt(q, kc_ref[...].astype(BF16))
        ends = lax.broadcasted_iota(jnp.int32, (1, n_cmp), 1) * CMP_STRIDE + (CMP_LEN - 1)
        s = s - slope * (float(tq) - ends.astype(F32))
        p_c = _masked_softmax(s, ends <= tq)
        oc_ref[...] = _dot(p_c.astype(BF16), vc_ref[...].astype(BF16))
        imp = p_c[0:SUBLANE] + p_c[SUBLANE:2 * SUBLANE] + p_c[2 * SUBLANE:3 * SUBLANE]
        imp = jnp.concatenate([imp, jnp.zeros((SUBLANE, sel_w - n_cmp), F32)], axis=-1)
        tq_col = jnp.full((SUBLANE, 1), tq, jnp.int32)
        sel = _select_blocks(_block_scores(imp, tq_col, n_sel), min(TOP_N, n_sel))
        for i in range(sel_w // LANE):
            sel_ref[i] = sel[:, i * LANE:(i + 1) * LANE]

        wb = win_ref.shape[1]
        kw = win_ref[0:D_KV_A, :].astype(BF16)
        vw = win_ref[D_KV_A:, :].astype(BF16)
        tk = (past_len - wb) + lax.broadcasted_iota(jnp.int32, (1, wb), 1)
        dist = tq - tk
        sw = _dot(q, kw) - slope * dist.astype(F32)
        mw = (dist >= 0) & (dist < WINDOW) & (tk >= 0)
        knew = winnew_ref[:, 0:D_KV_A].astype(BF16).astype(F32)
        vnew = winnew_ref[:, D_KV_A:].astype(BF16).astype(F32)
        s_new = jnp.sum(q.astype(F32) * knew, axis=-1, keepdims=True)
        sw = jnp.where(mw, sw, -jnp.inf)
        mx = jnp.maximum(jnp.max(sw, axis=-1, keepdims=True), s_new)
        e = jnp.where(mw, jnp.exp(sw - mx), 0.0)
        e_new = jnp.exp(s_new - mx)
        den = jnp.maximum(jnp.sum(e, axis=-1, keepdims=True) + e_new, 1e-30)
        p_w = e / den
        p_new = (e_new / den).astype(BF16).astype(F32)
        ow_ref[...] = _dot_nt(p_w.astype(BF16), vw) + p_new * vnew
        seq_lane = lax.broadcasted_iota(jnp.int32, winnewt_ref.shape, 1)
        new_col = jnp.sum(jnp.where(seq_lane == b, winnewt_ref[...], 0.0), axis=-1, keepdims=True)
        pos_lane = lax.broadcasted_iota(jnp.int32, (1, wb), 1)
        shifted = pltpu.roll(win_ref[...], wb - 1, axis=1)
        winout_ref[...] = jnp.where(pos_lane == wb - 1, new_col, shifted)

        m_ref[...] = jnp.full((ROWS_S, 1), NEG_BIG, F32)
        l_ref[...] = jnp.zeros((ROWS_S, 1), F32)
        acc_ref[...] = jnp.zeros((ROWS_S, MXU_DIM), F32)

    q = q_ref[...]
    step_keys = npg * page_size
    sel_tile = sel_ref[lax.shift_right_logical(j, 1)].astype(BF16)
    kmask = _dot(sel_tile, e2_ref[j & 1])
    kk = jnp.concatenate([pr[0:D_KV_A, :] for pr in page_refs], axis=1).astype(BF16)
    vv = jnp.concatenate([pr[D_KV_A:, :] for pr in page_refs], axis=1).astype(BF16)
    sc = _dot(q, kk)
    pos = j * step_keys + lax.broadcasted_iota(jnp.int32, (1, step_keys), 1)
    valid = (jnp.concatenate([kmask] * R_A, axis=0) > 0.5) & (pos <= tq)
    sc = jnp.where(valid, sc - slope * (float(tq) - pos.astype(F32)), NEG_BIG)
    m_old = m_ref[...]
    m_new = jnp.maximum(m_old, jnp.max(sc, axis=-1, keepdims=True))
    e = jnp.where(valid, jnp.exp(sc - m_new), 0.0)
    alpha = jnp.exp(m_old - m_new)
    l_ref[...] = alpha * l_ref[...] + jnp.sum(e, axis=-1, keepdims=True)
    acc_ref[...] = alpha * acc_ref[...] + _dot_nt(e.astype(BF16), vv)
    m_ref[...] = m_new

    @pl.when(j == n_steps - 1)
    def _():
        knew = new_ref[:, 2 * D_KV_A:3 * D_KV_A].astype(BF16).astype(F32)
        vnew = new_ref[:, 3 * D_KV_A:].astype(BF16).astype(F32)
        lane_new = 4 * (tq // SEL_BLOCK)
        sel_new = sel_ref[lane_new // LANE][:, lane_new % LANE:lane_new % LANE + 1]
        ok = jnp.concatenate([sel_new] * R_A, axis=0) > 0.5
        s_new = jnp.where(ok, jnp.sum(q.astype(F32) * knew, axis=-1, keepdims=True), NEG_BIG)
        m_old = m_ref[...]
        m_new = jnp.maximum(m_old, s_new)
        e_new = jnp.where(ok, jnp.exp(s_new - m_new), 0.0)
        alpha = jnp.exp(m_old - m_new)
        l_fin = alpha * l_ref[...] + e_new
        acc = alpha * acc_ref[...] + e_new.astype(BF16).astype(F32) * vnew
        o_s = acc / jnp.maximum(l_fin, 1e-30)

        grow = gate_ref[...]
        lane_g = lax.broadcasted_iota(jnp.int32, (ROWS_S, LANE), 1)
        col0 = 3 * (R_A * g_of_row + r_of_row)
        gsel = lambda c: jnp.sum(jnp.where(lane_g == col0 + c, grow, 0.0), axis=-1, keepdims=True)
        merged = gsel(0) * oc_ref[...] + gsel(1) * o_s + gsel(2) * ow_ref[...]
        merged = jnp.where(own_lanes & (g_of_row < G_A), merged, 0.0)
        for r in range(R_A):
            out_ref[:, r * MXU_DIM:(r + 1) * MXU_DIM] = jnp.sum(
                merged[r * SUBLANE:(r + 1) * SUBLANE], axis=0, keepdims=True)


def _nsa_sample(page_table, cache_t, layer, qa, gates, nsa_new, win_new, win_new_t, kc, vc, win_t, past_len):
    n_seq, n_pages = page_table.shape
    page_size = cache_t.shape[-1]
    npg = NSA_PAGES_PER_STEP
    n_steps = n_pages // npg
    wb = win_t.shape[-1]
    n_sel = (past_len + 1 + SEL_BLOCK - 1) // SEL_BLOCK
    sel_w = -(-(4 * n_sel) // LANE) * LANE
    tok = lambda w: pl.BlockSpec((None, 1, w), lambda b, j, pt: (b, 0, 0))

    def page_spec(k):
        return pl.BlockSpec((None, None, 2 * D_KV_A, page_size), lambda b, j, pt: (layer, pt[b, j * npg + k], 1, 0))

    cmp_spec = pl.BlockSpec((None, kc.shape[1], D_KV_A), lambda b, j, pt: (b, 0, 0))
    win_in_spec = pl.BlockSpec((None, None, 2 * D_KV_A, wb), lambda b, j, pt: (layer, b, 0, 0))
    win_out_spec = pl.BlockSpec((None, 2 * D_KV_A, wb), lambda b, j, pt: (b, 0, 0))
    newt_spec = pl.BlockSpec((2 * D_KV_A, n_seq), lambda b, j, pt: (0, 0))
    step_keys = npg * page_size
    lanes_per_step = 4 * step_keys // SEL_BLOCK
    assert 2 * lanes_per_step == LANE
    e2_np = np.zeros((2, LANE, step_keys), np.float32)
    for v in range(2):
        for k in range(step_keys):
            e2_np[v, v * lanes_per_step + 4 * (k // SEL_BLOCK), k] = 1.0
    e2 = jnp.asarray(e2_np, BF16)
    e2_spec = pl.BlockSpec(e2.shape, lambda b, j, pt: (0, 0, 0))
    grid_spec = pltpu.PrefetchScalarGridSpec(
        num_scalar_prefetch=1,
        grid=(n_seq, n_steps),
        in_specs=[page_spec(k) for k in range(npg)]
        + [tok(D_A), tok(LANE), tok(4 * D_KV_A), tok(2 * D_KV_A), newt_spec, cmp_spec, cmp_spec, win_in_spec, e2_spec],
        out_specs=[tok(D_A), win_out_spec],
        scratch_shapes=[pltpu.VMEM((ROWS_S, MXU_DIM), BF16), pltpu.VMEM((sel_w // LANE, SUBLANE, LANE), F32),
                        pltpu.VMEM((ROWS_S, 1), F32), pltpu.VMEM((ROWS_S, 1), F32),
                        pltpu.VMEM((ROWS_S, MXU_DIM), F32), pltpu.VMEM((ROWS_S, MXU_DIM), F32),
                        pltpu.VMEM((ROWS_S, MXU_DIM), F32)],
    )
    r3 = lambda a: a.reshape(n_seq, 1, a.shape[-1])
    oa, win_out = pl.pallas_call(
        functools.partial(_nsa_sample_kernel, past_len=past_len, page_size=page_size),
        grid_spec=grid_spec,
        out_shape=[jax.ShapeDtypeStruct((n_seq, 1, D_A), F32), jax.ShapeDtypeStruct((n_seq, 2 * D_KV_A, wb), F32)],
        compiler_params=_cparams(("arbitrary", "arbitrary")),
        name="nsa_sample",
    )(page_table, *([cache_t] * npg), r3(qa), r3(gates), r3(nsa_new), r3(win_new), win_new_t, kc, vc, win_t, e2)
    return oa.reshape(n_seq, D_A), win_out


ROWS_B = 16
SB_PAGES_PER_STEP = 8


def _sb_sample_kernel(pt_ref, *refs, past_len, page_size, n_pages):
    npg = SB_PAGES_PER_STEP
    page_refs = refs[:npg]
    q_ref, new_ref, u_ref, out_ref, qm_ref, carry_ref, acc_ref = refs[npg:]
    j = pl.program_id(1)
    n_steps = pl.num_programs(1)
    tq = past_len
    rowi = lax.broadcasted_iota(jnp.int32, (ROWS_B, D_B), 0)
    lane = lax.broadcasted_iota(jnp.int32, (ROWS_B, D_B), 1)
    own = lax.shift_right_logical(lane, 6) == rowi

    @pl.when(j == 0)
    def _():
        qm = jnp.where(own, q_ref[...] * QK_SCALE, 0.0)
        qm_ref[...] = qm.astype(BF16)
        acc_ref[...] = jnp.zeros(acc_ref.shape, F32)
        knew = new_ref[:, 0:D_B].astype(BF16).astype(F32)
        z = jnp.sum(qm.astype(BF16).astype(F32) * knew, axis=-1, keepdims=True)
        valid = jnp.full((ROWS_B, 1), past_len < tq)
        lf = jnp.where(valid, _log_sigmoid_neg(z), 0.0)
        a = jnp.where(valid, jnp.exp(_log_sigmoid_neg(z) + z), 0.0)
        acc_ref[...] = a.astype(BF16).astype(F32) * new_ref[:, D_B:].astype(BF16).astype(F32)
        carry_ref[...] = lf

    qm = qm_ref[...]
    u = u_ref[...]
    for c in range(npg // 2):
        newer, older = page_refs[2 * c], page_refs[2 * c + 1]
        pg_old = (n_pages - 1) - (j * npg + 2 * c + 1)
        kk = jnp.concatenate([older[0:D_B, :], newer[0:D_B, :]], axis=1).astype(BF16)
        vv = jnp.concatenate([older[D_B:, :], newer[D_B:, :]], axis=1).astype(BF16)
        tk = pg_old * page_size + lax.broadcasted_iota(jnp.int32, (1, 2 * page_size), 1)
        valid = tk < tq
        z = _dot(qm, kk)
        lf_all = _log_sigmoid_neg(z)
        lf = jnp.where(valid, lf_all, 0.0)
        later = carry_ref[...] + _split_dot(lf, u)
        a = jnp.where(valid, jnp.exp(lf_all + z + later), 0.0)
        acc_ref[...] = acc_ref[...] + _dot_nt(a.astype(BF16), vv)
        carry_ref[...] = carry_ref[...] + jnp.sum(lf, axis=-1, keepdims=True)

    @pl.when(j == n_steps - 1)
    def _():
        out_ref[...] = jnp.sum(jnp.where(own, acc_ref[...], 0.0), axis=0, keepdims=True)


def _sb_sample(page_table, cache_t, layer, qb, sb_new, umat, past_len):
    n_seq, n_pages = page_table.shape
    page_size = cache_t.shape[-1]
    npg = SB_PAGES_PER_STEP
    n_steps = n_pages // npg
    tok = lambda w: pl.BlockSpec((None, 1, w), lambda b, j, pt: (b, 0, 0))

    def page_spec(k):
        return pl.BlockSpec((None, None, 2 * D_B, page_size),
                            lambda b, j, pt: (layer, pt[b, n_pages - 1 - (j * npg + k)], 0, 0))

    grid_spec = pltpu.PrefetchScalarGridSpec(
        num_scalar_prefetch=1,
        grid=(n_seq, n_steps),
        in_specs=[page_spec(k) for k in range(npg)]
        + [tok(D_B), tok(2 * D_B), pl.BlockSpec((2 * page_size, 2 * page_size), lambda b, j, pt: (0, 0))],
        out_specs=tok(D_B),
        scratch_shapes=[pltpu.VMEM((ROWS_B, D_B), BF16), pltpu.VMEM((ROWS_B, 1), F32), pltpu.VMEM((ROWS_B, D_B), F32)],
    )
    r3 = lambda a: a.reshape(n_seq, 1, a.shape[-1])
    out = pl.pallas_call(
        functools.partial(_sb_sample_kernel, past_len=past_len, page_size=page_size, n_pages=n_pages),
        grid_spec=grid_spec,
        out_shape=jax.ShapeDtypeStruct((n_seq, 1, D_B), F32),
        compiler_params=_cparams(("arbitrary", "arbitrary")),
        name="sb_sample",
    )(page_table, *([cache_t] * npg), r3(qb), r3(sb_new), umat)
    return out.reshape(n_seq, D_B)


def _oa_perm():
    perm = np.zeros((D_A,), np.int32)
    for r in range(R_A):
        for g in range(G_A):
            for d in range(HEAD_DIM):
                perm[r * MXU_DIM + g * HEAD_DIM + d] = g * R_A * HEAD_DIM + r * HEAD_DIM + d
    return perm


def _place_mats():
    place = np.zeros((H_A, LANE, MXU_DIM), np.float32)
    for g in range(G_A):
        for r in range(R_A):
            h = R_A * g + r
            for d in range(HEAD_DIM):
                place[h, (h % 2) * HEAD_DIM + d, g * HEAD_DIM + d] = 1.0
    return place


def _expand_mat(t):
    e = np.zeros((G_A, Q_TILE, t), np.float32)
    for g in range(G_A):
        for k in range(t):
            e[g, 4 * (k // SEL_BLOCK) + g, k] = 1.0
    return e


def _prep_layer(l, w_in, w_out, norm_mix, norm_ffn, q_norm, k_norm, cmp_pos, cmp_w1, cmp_w2,
                out_norm_a, out_norm_b, conv_w, conv_b, conv_ln_g, conv_ln_b, w_gate, w_up, w_down):
    wi = w_in[l]
    w_tok = jnp.concatenate(
        [wi[:, :S_KV], wi[:, S_QB:S_KB], wi[:, S_CIN:], wi[:, S_GATE:S_QB], jnp.zeros((D_MODEL, LANE - N_GATE), F32),
         wi[:, S_KV:S_KV + 2 * D_KV_A]], axis=1).astype(BF16)
    wt = jnp.transpose(wi)
    w_feat = jnp.concatenate([wt[S_KV:S_GATE], wt[S_KB:S_CIN]], axis=0).astype(BF16)
    perm = _oa_perm()
    wo = w_out[l]
    w_out_p = jnp.concatenate([wo[:D_A][perm], wo[D_A:]], axis=0).astype(BF16)
    eye = jnp.eye(G_A, dtype=F32)
    w1 = cmp_w1[l].reshape(2, CMP_LEN, HEAD_DIM, HEAD_DIM)
    w1bd = jnp.einsum('gh,kpde->kpgdhe', eye, w1).reshape(2, 2, CMP_STRIDE * D_KV_A, D_KV_A).astype(BF16)
    w2bd = jnp.einsum('gh,kde->kgdhe', eye, cmp_w2[l]).reshape(2, D_KV_A, D_KV_A).astype(BF16)
    tile4 = lambda v: jnp.tile(v, G_A).reshape(1, D_KV_A)
    col4 = lambda v: jnp.tile(v, G_A).reshape(D_KV_A, 1)
    return dict(
        w_tok=w_tok, w_feat=w_feat, w_out_p=w_out_p, w1bd=w1bd, w2bd=w2bd,
        pos4=jnp.tile(cmp_pos[l], (1, 1, G_A)),
        gn_mix=norm_mix[l].reshape(1, D_MODEL), gn_ffn=norm_ffn[l].reshape(1, D_MODEL),
        qn4=tile4(q_norm[l]), kn0_4=tile4(k_norm[l, 0]), kn1_4=tile4(k_norm[l, 1]), kn2_4=tile4(k_norm[l, 2]),
        kn1c=col4(k_norm[l, 1]), kn2c=col4(k_norm[l, 2]),
        ga=out_norm_a[l][perm].reshape(1, D_A), gb=out_norm_b[l].reshape(1, D_B),
        cw=conv_w[l], cb=conv_b[l].reshape(1, D_C), lg=conv_ln_g[l].reshape(1, D_C), lb=conv_ln_b[l].reshape(1, D_C),
        wg=w_gate[l].astype(BF16), wu=w_up[l].astype(BF16), wd=w_down[l].astype(BF16),
    )


def kernel(x_prompt, x_sample, cache_nsa, cache_sb, state_win, state_conv, page_table, w_in, w_out, norm_mix,
           norm_ffn, q_norm, k_norm, cmp_pos, cmp_w1, cmp_w2, out_norm_a, out_norm_b, conv_w, conv_b, conv_ln_g,
           conv_ln_b, w_gate, w_up, w_down):
    b, t, _ = x_prompt.shape
    n_seq, dec_seq, _ = x_sample.shape
    assert dec_seq == 1 and t % SB_TILE == 0 and t % (CMP_STRIDE * SUBLANE) == 0 and t >= WINDOW
    n_pool, page_size = cache_nsa.shape[1], cache_nsa.shape[2]
    n_pages = page_table.shape[1]
    past_len = n_pages * page_size
    assert page_size % CMP_STRIDE == 0 and page_size // CMP_STRIDE == SUBLANE
    assert n_pages % CMP_PAGES_PER_STEP == 0 and n_pages % NSA_PAGES_PER_STEP == 0 and n_pages % SB_PAGES_PER_STEP == 0

    pmat = jnp.asarray(np.kron(np.eye(G_A, dtype=np.float32), np.full((HEAD_DIM, HEAD_DIM), 1.0 / HEAD_DIM, np.float32)), BF16)
    place = jnp.asarray(_place_mats(), BF16)
    expand = jnp.asarray(_expand_mat(t), BF16)
    u_prompt = jnp.asarray(np.tril(np.ones((SB_TILE, SB_TILE), np.float32), -1), BF16)
    u_page = jnp.asarray(np.tril(np.ones((2 * page_size, 2 * page_size), np.float32), -1), BF16)

    fmaj = lambda a: jnp.transpose(a, (0, 1, 3, 4, 5, 2))
    cache_nsa_t = fmaj(cache_nsa).reshape(DEPTH, n_pool, 4 * D_KV_A, page_size)
    cache_sb_t = fmaj(cache_sb).reshape(DEPTH, n_pool, 2 * D_B, page_size)
    wb = state_win.shape[2]
    win_t = fmaj(state_win).reshape(DEPTH, n_seq, 2 * D_KV_A, wb)
    conv_t = jnp.transpose(state_conv, (0, 2, 1, 3))
    tmaj = lambda a, lead: jnp.transpose(a.reshape(a.shape[0], *lead, a.shape[-1]), (0, len(lead) + 1, *range(1, len(lead) + 1)))

    xp = x_prompt.reshape(b * t, D_MODEL)
    xs = x_sample.reshape(n_seq, D_MODEL)
    outs = {k: [] for k in ("p_nsa", "p_sb", "p_win", "p_conv", "s_nsa", "s_sb", "s_win", "s_conv")}
    for l in range(DEPTH):
        w = _prep_layer(l, w_in, w_out, norm_mix, norm_ffn, q_norm, k_norm, cmp_pos, cmp_w1, cmp_w2, out_norm_a,
                        out_norm_b, conv_w, conv_b, conv_ln_g, conv_ln_b, w_gate, w_up, w_down)

        qa, nsa_t, winp_t, qb, sb_t, cin, gates, cmp_rows = _in_proj_prompt(
            xp, w["gn_mix"], w["w_tok"], w["w_feat"], w["qn4"], w["kn1c"], w["kn2c"], pmat, b, t)
        kc, vc = _compress_prompt(cmp_rows, w["pos4"], w["w1bd"], w["w2bd"], w["kn0_4"], pmat, b, t)
        oa = _nsa_prompt(qa, gates, kc, vc, nsa_t, winp_t, place, expand, b, t)
        ob = _sb_prompt(qb, sb_t, u_prompt, b, t)
        cv, conv_state = _conv_prompt(cin, w["cw"], w["cb"], w["lg"], w["lb"], b, t)
        xp = _out_proj(oa, ob, cv, xp, w["ga"], w["gb"], w["w_out_p"], tm=512)
        xp = _ffn(xp, w["gn_ffn"], w["wg"], w["wu"], w["wd"], tm=1024, tf=512, tn=512)
        outs["p_nsa"].append(tmaj(nsa_t, (4, G_A, HEAD_DIM)))
        outs["p_sb"].append(tmaj(sb_t, (2, H_B, HEAD_DIM)))
        outs["p_win"].append(tmaj(winp_t[:, :, t - min(WINDOW, t):], (2, G_A, HEAD_DIM)))
        outs["p_conv"].append(conv_state)

        qa, nsa_new, win_new, qb, sb_new, cin, gates, win_new_t = _in_proj_sample(
            xs, w["gn_mix"], w["w_tok"], w["w_feat"], w["qn4"], w["kn1_4"], w["kn2_4"], w["kn2c"], pmat)
        kc, vc = _compress_sample(page_table, cache_nsa_t, l, w["pos4"], w["w1bd"], w["w2bd"], w["kn0_4"], pmat)
        oa, win_out_t = _nsa_sample(page_table, cache_nsa_t, l, qa, gates, nsa_new, win_new, win_new_t, kc, vc,
                                    win_t, past_len)
        ob = _sb_sample(page_table, cache_sb_t, l, qb, sb_new, u_page, past_len)
        cv, conv_state_t = _conv_sample(cin, conv_t, l, w["cw"], w["cb"], w["lg"], w["lb"])
        xs = _out_proj(oa, ob, cv, xs, w["ga"], w["gb"], w["w_out_p"], tm=n_seq)
        xs = _ffn(xs, w["gn_ffn"], w["wg"], w["wu"], w["wd"], tm=n_seq, tf=512, tn=512)
        outs["s_nsa"].append(nsa_new.reshape(n_seq, 1, 4, G_A, HEAD_DIM))
        outs["s_sb"].append(sb_new.reshape(n_seq, 1, 2, H_B, HEAD_DIM))
        outs["s_win"].append(tmaj(win_out_t, (2, G_A, HEAD_DIM)))
        outs["s_conv"].append(jnp.transpose(conv_state_t, (1, 0, 2)))

    st = lambda k: jnp.stack(outs[k])
    return (xp.reshape(b, t, D_MODEL), xs.reshape(n_seq, 1, D_MODEL), st("p_nsa"), st("p_sb"), st("p_win"),
            st("p_conv"), st("s_nsa"), st("s_sb"), st("s_win"), st("s_conv"))
```

```python
import functools

import numpy as np
import jax
import jax.numpy as jnp
from jax import lax
from jax.experimental import pallas as pl
from jax.experimental.pallas import tpu as pltpu

F32 = jnp.float32
BF16 = jnp.bfloat16

LANE = 128
SUBLANE = 8
MXU_DIM = 256
VMEM_LIMIT_BYTES = 56 * 1024 * 1024

D_MODEL = 2048
DEPTH = 2
HEAD_DIM = 64
H_A = 12
G_A = 4
R_A = H_A // G_A
H_B = 12
D_A = H_A * HEAD_DIM
D_KV_A = G_A * HEAD_DIM
D_B = H_B * HEAD_DIM
D_C = D_MODEL - D_A - D_B
CONV_W = 31
CMP_LEN = 32
CMP_STRIDE = 16
SEL_BLOCK = 64
TOP_N = 16
WINDOW = 512
FORCE_BONUS = 1000.0
N_GATE = 3 * H_A
D_IN = D_A + 6 * D_KV_A + N_GATE + 3 * D_B + 2 * D_C
D_FF = -(-8 * D_MODEL // (3 * 256)) * 256
EPS = 1e-6
QK_SCALE = HEAD_DIM ** -0.5
NEG_BIG = -1e30

S_KV = D_A
S_GATE = S_KV + 6 * D_KV_A
S_QB = S_GATE + N_GATE
S_KB = S_QB + D_B
S_CIN = S_KB + 2 * D_B
T_QA = 0
T_QB = T_QA + D_A
T_CIN = T_QB + D_B
T_GATE = T_CIN + 2 * D_C
T_CMP = T_GATE + LANE
W_TOK = T_CMP + 2 * D_KV_A
F_KV = 0
F_SB = F_KV + 6 * D_KV_A
W_FEAT = F_SB + 2 * D_B

SLOPES = (2.0 ** (-8.0 * np.arange(1, H_A + 1, dtype=np.float32) / H_A)).astype(np.float32).reshape(G_A, R_A)


def _cparams(sem):
    return pltpu.CompilerParams(dimension_semantics=sem, vmem_limit_bytes=VMEM_LIMIT_BYTES)


def _const_spec(shape):
    nd = len(shape)
    return pl.BlockSpec(shape, lambda *_: (0,) * nd)


def _dot(a, b):
    return jnp.dot(a, b, preferred_element_type=F32)


def _dot_nt(a, b):
    return lax.dot_general(a, b, (((1,), (1,)), ((), ())), preferred_element_type=F32)


def _split_dot(x, w_bf16):
    hi = x.astype(BF16)
    lo = (x - hi.astype(F32)).astype(BF16)
    return _dot(hi, w_bf16) + _dot(lo, w_bf16)


def _sigmoid(x):
    return 1.0 / (1.0 + jnp.exp(-x))


def _rmsnorm_rows(x, gain):
    ms = jnp.mean(x * x, axis=-1, keepdims=True)
    return x * lax.rsqrt(ms + EPS) * gain


def _head_rmsnorm(u, gain_tile, pmat):
    outs = []
    for c in range(u.shape[1] // MXU_DIM):
        uc = u[:, c * MXU_DIM:(c + 1) * MXU_DIM]
        ms = _split_dot(uc * uc, pmat)
        outs.append(uc * lax.rsqrt(ms + EPS) * gain_tile)
    return outs[0] if len(outs) == 1 else jnp.concatenate(outs, axis=-1)


def _masked_softmax(s, mask):
    s = jnp.where(mask, s, -jnp.inf)
    m = jnp.max(s, axis=-1, keepdims=True)
    m = jnp.where(m == -jnp.inf, 0.0, m)
    e = jnp.where(mask, jnp.exp(s - m), 0.0)
    return e / jnp.maximum(jnp.sum(e, axis=-1, keepdims=True), 1e-30)


def _log_sigmoid_neg(z):
    return -(jnp.maximum(z, 0.0) + jnp.log(1.0 + jnp.exp(-jnp.abs(z))))


def _gelu_tanh(x):
    return 0.5 * x * (1.0 + jnp.tanh(np.float32(np.sqrt(2.0 / np.pi)) * (x + 0.044715 * (x * x * x))))


def _head_rmsnorm_t(u, gain_col, pmat):
    sq = u * u
    hi = sq.astype(BF16)
    lo = (sq - hi.astype(F32)).astype(BF16)
    ms = _dot(pmat, hi) + _dot(pmat, lo)
    return u * lax.rsqrt(ms + EPS) * gain_col


def _in_proj_prompt_kernel(x_ref, gn_ref, wtok_ref, wfeat_ref, qn_ref, kn1c_ref, kn2c_ref, pmat_ref,
                           qa_ref, nsat_ref, wint_ref, qb_ref, sbt_ref, cin_ref, gate_ref, cmp_ref):
    xn = _rmsnorm_rows(x_ref[...], gn_ref[...]).astype(BF16)
    pmat = pmat_ref[...]
    tok = lambda lo, hi: _dot(xn, wtok_ref[:, lo:hi])
    feat = lambda lo, hi: _dot_nt(wfeat_ref[lo:hi, :], xn)

    qa_ref[...] = _head_rmsnorm(tok(T_QA, T_QB), qn_ref[...], pmat)
    kvt = feat(F_KV, F_SB)
    nsat_ref[0:2 * D_KV_A, :] = kvt[0:2 * D_KV_A]
    nsat_ref[2 * D_KV_A:3 * D_KV_A, :] = _head_rmsnorm_t(kvt[2 * D_KV_A:3 * D_KV_A], kn1c_ref[...], pmat)
    nsat_ref[3 * D_KV_A:4 * D_KV_A, :] = kvt[3 * D_KV_A:4 * D_KV_A]
    wint_ref[0:D_KV_A, :] = _head_rmsnorm_t(kvt[4 * D_KV_A:5 * D_KV_A], kn2c_ref[...], pmat)
    wint_ref[D_KV_A:2 * D_KV_A, :] = kvt[5 * D_KV_A:6 * D_KV_A]
    qb_ref[...] = tok(T_QB, T_CIN)
    sbt_ref[...] = feat(F_SB, W_FEAT)
    cin_ref[...] = tok(T_CIN, T_GATE)
    gate_ref[...] = _sigmoid(tok(T_GATE, T_CMP))
    cmp_ref[...] = tok(T_CMP, W_TOK)


IN_PROJ_TILE = 256


def _resident_spec(shape):
    return pl.BlockSpec(shape, lambda *_: (0,) * len(shape), pipeline_mode=pl.Buffered(1))


def _in_proj_prompt(x2d, gn, w_tok, w_feat, qn4, kn1c, kn2c, pmat, b, t):
    tm = IN_PROJ_TILE
    nt = t // tm
    row = lambda w: pl.BlockSpec((tm, w), lambda i: (i, 0))
    fmaj = lambda f: pl.BlockSpec((None, f, tm), lambda i: (i // nt, 0, i % nt))
    tok_out = lambda w: jax.ShapeDtypeStruct((b * t, w), F32)
    feat_out = lambda f: jax.ShapeDtypeStruct((b, f, t), F32)
    col = _const_spec((MXU_DIM, 1))
    return pl.pallas_call(
        _in_proj_prompt_kernel,
        grid=(b * nt,),
        in_specs=[row(D_MODEL), _const_spec((1, D_MODEL)),
                  _resident_spec((D_MODEL, W_TOK)), _resident_spec((W_FEAT, D_MODEL)),
                  _const_spec((1, MXU_DIM)), col, col, _const_spec((MXU_DIM, MXU_DIM))],
        out_specs=[row(D_A), fmaj(4 * D_KV_A), fmaj(2 * D_KV_A), row(D_B), fmaj(2 * D_B), row(2 * D_C), row(LANE),
                   row(2 * D_KV_A)],
        out_shape=[tok_out(D_A), feat_out(4 * D_KV_A), feat_out(2 * D_KV_A), tok_out(D_B), feat_out(2 * D_B),
                   tok_out(2 * D_C), tok_out(LANE), tok_out(2 * D_KV_A)],
        compiler_params=_cparams(("arbitrary",)),
        name="in_proj_prompt",
    )(x2d, gn, w_tok, w_feat, qn4, kn1c, kn2c, pmat)


def _in_proj_sample_kernel(x_ref, gn_ref, wtok_ref, wfeat_ref, qn_ref, kn1_ref, kn2_ref, kn2c_ref, pmat_ref,
                           qa_ref, nsa_ref, win_ref, qb_ref, sb_ref, cin_ref, gate_ref, wint_ref):
    xn = _rmsnorm_rows(x_ref[...], gn_ref[...]).astype(BF16)
    pmat = pmat_ref[...]
    tok = lambda lo, hi: _dot(xn, wtok_ref[:, lo:hi])
    tok_f = lambda lo, hi: _dot_nt(xn, wfeat_ref[lo:hi, :])

    qa_ref[...] = _head_rmsnorm(tok(T_QA, T_QB), qn_ref[...], pmat)
    kv = tok_f(F_KV, F_SB)
    nsa_ref[:, 0:2 * D_KV_A] = kv[:, 0:2 * D_KV_A]
    nsa_ref[:, 2 * D_KV_A:3 * D_KV_A] = _head_rmsnorm(kv[:, 2 * D_KV_A:3 * D_KV_A], kn1_ref[...], pmat)
    nsa_ref[:, 3 * D_KV_A:4 * D_KV_A] = kv[:, 3 * D_KV_A:4 * D_KV_A]
    win_ref[:, 0:D_KV_A] = _head_rmsnorm(kv[:, 4 * D_KV_A:5 * D_KV_A], kn2_ref[...], pmat)
    win_ref[:, D_KV_A:2 * D_KV_A] = kv[:, 5 * D_KV_A:6 * D_KV_A]
    qb_ref[...] = tok(T_QB, T_CIN)
    sb_ref[...] = tok_f(F_SB, W_FEAT)
    cin_ref[...] = tok(T_CIN, T_GATE)
    gate_ref[...] = _sigmoid(tok(T_GATE, T_CMP))
    wt_win = _dot_nt(wfeat_ref[F_KV + 4 * D_KV_A:F_SB, :], xn)
    wint_ref[0:D_KV_A, :] = _head_rmsnorm_t(wt_win[0:D_KV_A], kn2c_ref[...], pmat)
    wint_ref[D_KV_A:, :] = wt_win[D_KV_A:]


def _in_proj_sample(x2d, gn, w_tok, w_feat, qn4, kn1_4, kn2_4, kn2c, pmat):
    m = x2d.shape[0]
    widths = (D_A, 4 * D_KV_A, 2 * D_KV_A, D_B, 2 * D_B, 2 * D_C, LANE)
    full = lambda r, c: _const_spec((r, c))
    return pl.pallas_call(
        _in_proj_sample_kernel,
        grid=(1,),
        in_specs=[full(m, D_MODEL), full(1, D_MODEL),
                  _resident_spec((D_MODEL, W_TOK)), _resident_spec((W_FEAT, D_MODEL)),
                  full(1, MXU_DIM), full(1, MXU_DIM), full(1, MXU_DIM), full(MXU_DIM, 1), full(MXU_DIM, MXU_DIM)],
        out_specs=[full(m, w) for w in widths] + [full(2 * D_KV_A, m)],
        out_shape=[jax.ShapeDtypeStruct((m, w), F32) for w in widths]
        + [jax.ShapeDtypeStruct((2 * D_KV_A, m), F32)],
        compiler_params=_cparams(("arbitrary",)),
        name="in_proj_sample",
    )(x2d, gn, w_tok, w_feat, qn4, kn1_4, kn2_4, kn2c, pmat)


def _out_proj_kernel(oa_ref, ob_ref, cv_ref, x_ref, ga_ref, gb_ref, w_ref, y_ref):
    ha = _rmsnorm_rows(oa_ref[...], ga_ref[...]).astype(BF16)
    hb = _rmsnorm_rows(ob_ref[...], gb_ref[...]).astype(BF16)
    hc = cv_ref[...].astype(BF16)
    y_ref[...] = x_ref[...] + _dot(jnp.concatenate([ha, hb, hc], axis=-1), w_ref[...])


def _out_proj(oa, ob, cv, x2d, ga, gb, w_out_p, tm):
    m = x2d.shape[0]
    row = lambda w: pl.BlockSpec((tm, w), lambda i: (i, 0))
    return pl.pallas_call(
        _out_proj_kernel,
        grid=(m // tm,),
        in_specs=[row(D_A), row(D_B), row(D_C), row(D_MODEL), _const_spec((1, D_A)), _const_spec((1, D_B)),
                  _const_spec((D_MODEL, D_MODEL))],
        out_specs=row(D_MODEL),
        out_shape=jax.ShapeDtypeStruct((m, D_MODEL), F32),
        compiler_params=_cparams(("arbitrary",)),
        name="out_proj",
    )(oa, ob, cv, x2d, ga, gb, w_out_p)


def _ffn_up_kernel(x_ref, gn_ref, wg_ref, wu_ref, h_ref, xn_ref):
    @pl.when(pl.program_id(1) == 0)
    def _():
        xn_ref[...] = _rmsnorm_rows(x_ref[...], gn_ref[...]).astype(BF16)

    xn = xn_ref[...]
    a = _dot(xn, wg_ref[...])
    b = _dot(xn, wu_ref[...])
    h_ref[...] = (a * _sigmoid(a) * b).astype(BF16)


def _ffn_down_kernel(h_ref, x_ref, wd_ref, y_ref):
    y_ref[...] = x_ref[...] + _dot(h_ref[...], wd_ref[...])


def _ffn(x2d, gn, wg, wu, wd, tm, tf, tn):
    m = x2d.shape[0]
    h = pl.pallas_call(
        _ffn_up_kernel,
        grid=(m // tm, D_FF // tf),
        in_specs=[pl.BlockSpec((tm, D_MODEL), lambda i, j: (i, 0)), _const_spec((1, D_MODEL)),
                  pl.BlockSpec((D_MODEL, tf), lambda i, j: (0, j)), pl.BlockSpec((D_MODEL, tf), lambda i, j: (0, j))],
        out_specs=pl.BlockSpec((tm, tf), lambda i, j: (i, j)),
        out_shape=jax.ShapeDtypeStruct((m, D_FF), BF16),
        scratch_shapes=[pltpu.VMEM((tm, D_MODEL), BF16)],
        compiler_params=_cparams(("arbitrary", "arbitrary")),
        name="ffn_up",
    )(x2d, gn, wg, wu)
    return pl.pallas_call(
        _ffn_down_kernel,
        grid=(m // tm, D_MODEL // tn),
        in_specs=[pl.BlockSpec((tm, D_FF), lambda i, j: (i, 0)), pl.BlockSpec((tm, tn), lambda i, j: (i, j)),
                  pl.BlockSpec((D_FF, tn), lambda i, j: (0, j))],
        out_specs=pl.BlockSpec((tm, tn), lambda i, j: (i, j)),
        out_shape=jax.ShapeDtypeStruct((m, D_MODEL), F32),
        compiler_params=_cparams(("arbitrary", "arbitrary")),
        name="ffn_down",
    )(h, x2d, wd)


def _compress_rows(get_piece, n_rows, pos_ref, w1_ref, w2_ref, kn0_ref, pmat_ref):
    outs = []
    for kv in range(2):
        pieces = [get_piece(p, kv) for p in range(CMP_STRIDE)]
        lhs = [jnp.concatenate([(pieces[p] + pos_ref[kv, half * CMP_STRIDE + p:half * CMP_STRIDE + p + 1, :]).astype(BF16)
                                for p in range(CMP_STRIDE)], axis=-1) for half in range(2)]
        first = _dot(lhs[0], w1_ref[kv, 0])
        second = _dot(lhs[1], w1_ref[kv, 1])
        pre = first + pltpu.roll(second, n_rows - 1, axis=0)
        o = _dot(_gelu_tanh(pre).astype(BF16), w2_ref[kv])
        if kv == 0:
            o = _head_rmsnorm(o, kn0_ref[...], pmat_ref[...])
        outs.append(o)
    return outs


def _compress_prompt_kernel(s0_ref, s1_ref, s2_ref, s3_ref, pos_ref, w1_ref, w2_ref, kn0_ref, pmat_ref,
                            kc_ref, vc_ref):
    slabs = (s0_ref, s1_ref, s2_ref, s3_ref)
    n_rows = s0_ref.shape[0] // CMP_STRIDE

    def piece(p, kv):
        return jnp.concatenate([slabs[2 * kv + h][pl.ds(p, n_rows, stride=CMP_STRIDE), :] for h in range(2)], axis=-1)

    kc, vc = _compress_rows(piece, n_rows, pos_ref, w1_ref, w2_ref, kn0_ref, pmat_ref)
    kc_ref[...] = kc
    vc_ref[...] = vc


def _compress_prompt(cmp_rows, pos4, w1bd, w2bd, kn0_4, pmat, b, t):
    nsub = t // CMP_STRIDE
    n_slab = 2 * D_KV_A // LANE
    return pl.pallas_call(
        _compress_prompt_kernel,
        grid=(b,),
        in_specs=[pl.BlockSpec((t, LANE), lambda i, s=s: (i, s)) for s in range(n_slab)]
        + [_const_spec(pos4.shape), _const_spec(w1bd.shape), _const_spec(w2bd.shape),
           _const_spec((1, MXU_DIM)), _const_spec((MXU_DIM, MXU_DIM))],
        out_specs=[pl.BlockSpec((None, nsub, D_KV_A), lambda i: (i, 0, 0))] * 2,
        out_shape=[jax.ShapeDtypeStruct((b, nsub, D_KV_A), F32)] * 2,
        compiler_params=_cparams(("arbitrary",)),
        name="compress_prompt",
    )(*([cmp_rows] * n_slab), pos4, w1bd, w2bd, kn0_4, pmat)


CMP_PAGES_PER_STEP = 32


def _compress_sample_kernel(pt_ref, *refs, page_size):
    npg = CMP_PAGES_PER_STEP
    page_refs = refs[:npg + 1]
    pos_ref, w1_ref, w2_ref, kn0_ref, pmat_ref, kc_ref, vc_ref, rows_ref = refs[npg + 1:]
    sub = page_size // CMP_STRIDE
    n_rows = (npg + 1) * sub
    n_slab = 2 * D_KV_A // LANE

    for k, pr in enumerate(page_refs):
        for s in range(n_slab):
            rows_ref[s, k * page_size:(k + 1) * page_size, :] = pr[s * LANE:(s + 1) * LANE, :].T

    def piece(p, kv):
        slabs = [rows_ref[2 * kv + h, pl.ds(p, n_rows, stride=CMP_STRIDE), :] for h in range(2)]
        return jnp.concatenate(slabs, axis=-1)

    kc, vc = _compress_rows(piece, n_rows, pos_ref, w1_ref, w2_ref, kn0_ref, pmat_ref)
    kc_ref[...] = kc[0:npg * sub]
    vc_ref[...] = vc[0:npg * sub]


def _compress_sample(page_table, cache_t, layer, pos4, w1bd, w2bd, kn0_4, pmat):
    n_seq, n_pages = page_table.shape
    page_size = cache_t.shape[-1]
    npg = CMP_PAGES_PER_STEP
    n_steps = n_pages // npg
    sub = page_size // CMP_STRIDE

    def page_spec(k):
        def imap(b, j, pt):
            return (layer, pt[b, jnp.minimum(j * npg + k, n_pages - 1)], 0, 0)
        return pl.BlockSpec((None, None, 2 * D_KV_A, page_size), imap)

    const = lambda shape: pl.BlockSpec(shape, lambda b, j, pt: (0,) * len(shape))
    out_spec = pl.BlockSpec((None, npg * sub, D_KV_A), lambda b, j, pt: (b, j, 0))
    grid_spec = pltpu.PrefetchScalarGridSpec(
        num_scalar_prefetch=1,
        grid=(n_seq, n_steps),
        in_specs=[page_spec(k) for k in range(npg + 1)]
        + [const(pos4.shape), const(w1bd.shape), const(w2bd.shape), const((1, MXU_DIM)), const((MXU_DIM, MXU_DIM))],
        out_specs=[out_spec, out_spec],
        scratch_shapes=[pltpu.VMEM((2 * D_KV_A // LANE, (npg + 1) * page_size, LANE), F32)],
    )
    return pl.pallas_call(
        functools.partial(_compress_sample_kernel, page_size=page_size),
        grid_spec=grid_spec,
        out_shape=[jax.ShapeDtypeStruct((n_seq, n_pages * sub, D_KV_A), F32)] * 2,
        compiler_params=_cparams(("arbitrary", "arbitrary")),
        name="compress_sample",
    )(page_table, *([cache_t] * (npg + 1)), pos4, w1bd, w2bd, kn0_4, pmat)


def _select_blocks(score, n_top):
    width = score.shape[1]
    per = SEL_BLOCK // CMP_STRIDE
    lane = lax.broadcasted_iota(jnp.int32, (1, width), 1)
    rank = jnp.zeros(score.shape, F32)
    for k in range(1, width // per):
        other = pltpu.roll(score, per * k, axis=1)
        from_lower = jnp.where(lane >= per * k, 1.0, 0.0)
        rank = rank + jnp.where(other > score, 1.0, jnp.where(other == score, from_lower, 0.0))
    return jnp.where(rank < n_top, jnp.where(score >= 0.0, 1.0, 0.0), 0.0)


def _block_scores(imp, tq_col, n_sel):
    width = imp.shape[1]
    per = SEL_BLOCK // CMP_STRIDE
    pooled = imp
    for k in range(1, per):
        pooled = pooled + pltpu.roll(imp, width - k, axis=1)
    lane = lax.broadcasted_iota(jnp.int32, imp.shape, 1)
    blk = lax.shift_right_logical(lane, 2)
    cur = lax.shift_right_logical(tq_col, 6)
    valid = blk <= cur
    forced = (blk == 0) | (blk == cur) | (blk == cur - 1)
    score = jnp.where(valid, jnp.where(forced, FORCE_BONUS, pooled), -1.0)
    cand = ((lane & (per - 1)) == 0) & (blk < n_sel)
    return jnp.where(cand, score, -jnp.inf)


Q_TILE = 128
SEL_CHUNK = 512
N_WIN_BLOCKS = WINDOW // Q_TILE + 1


def _nsa_prompt_kernel(qa_ref, gate_ref, kc_ref, vc_ref, ksv_ref, w0_ref, w1_ref, w2_ref, w3_ref, w4_ref,
                       place_ref, expand_ref, out_ref, s_ref, mx_ref, l_ref, acc_ref):
    win_refs = (w0_ref, w1_ref, w2_ref, w3_ref, w4_ref)
    t_len = ksv_ref.shape[1]
    n_cmp = kc_ref.shape[0]
    n_sel = t_len // SEL_BLOCK
    n_chunk_total = t_len // SEL_CHUNK
    qi = pl.program_id(1)
    start = qi * Q_TILE
    rows3 = R_A * Q_TILE
    rowi = lax.broadcasted_iota(jnp.int32, (rows3, 1), 0)
    r_of_row = lax.shift_right_logical(rowi, 7)
    tq3 = start + (rowi & (Q_TILE - 1))
    tq3f = tq3.astype(F32)
    tq1 = tq3[0:Q_TILE]
    lane_out = lax.broadcasted_iota(jnp.int32, (Q_TILE, MXU_DIM), 1)
    n_chunks = lax.shift_right_logical(start + Q_TILE + SEL_CHUNK - 1, 9)

    q3s, slopes, o_cs, scores = [], [], [], []
    kc = kc_ref[...].astype(BF16)
    vc = vc_ref[...].astype(BF16)
    ends = lax.broadcasted_iota(jnp.int32, (1, n_cmp), 1) * CMP_STRIDE + (CMP_LEN - 1)
    for g in range(G_A):
        qs = []
        for r in range(R_A):
            h = R_A * g + r
            qblk = qa_ref[:, (h // 2) * LANE:(h // 2 + 1) * LANE].astype(BF16)
            qs.append(_dot(qblk, place_ref[h]) * QK_SCALE)
        q3 = jnp.concatenate(qs, axis=0).astype(BF16)
        slope3 = jnp.where(r_of_row == 0, float(SLOPES[g, 0]),
                           jnp.where(r_of_row == 1, float(SLOPES[g, 1]), float(SLOPES[g, 2])))
        s = _dot_nt(q3, kc) - slope3 * (tq3f - ends.astype(F32))
        p_c = _masked_softmax(s, ends <= tq3)
        o_cs.append(_dot(p_c.astype(BF16), vc))
        imp = p_c[0:Q_TILE] + p_c[Q_TILE:2 * Q_TILE] + p_c[2 * Q_TILE:3 * Q_TILE]
        scores.append(_block_scores(imp, tq1, n_sel))
        q3s.append(q3)
        slopes.append(slope3)

    lane_q = lax.broadcasted_iota(jnp.int32, (1, n_cmp), 1)
    combined = scores[0]
    for g in range(1, G_A):
        combined = jnp.where((lane_q & (G_A - 1)) == g, pltpu.roll(scores[g], g, axis=1), combined)
    sel = _select_blocks(combined, min(TOP_N, n_sel)).astype(BF16)

    mx_ref[...] = jnp.full(mx_ref.shape, NEG_BIG, F32)
    l_ref[...] = jnp.zeros(l_ref.shape, F32)
    acc_ref[...] = jnp.zeros(acc_ref.shape, F32)
    for c in range(n_chunk_total):
        lo, hi = c * SEL_CHUNK, (c + 1) * SEL_CHUNK

        @pl.when(c < n_chunks)
        def _(lo=lo, hi=hi):
            kk = ksv_ref[0:D_KV_A, lo:hi].astype(BF16)
            pos = lo + lax.broadcasted_iota(jnp.int32, (1, SEL_CHUNK), 1)
            dist = tq3f - pos.astype(F32)
            causal = pos <= tq3
            for g in range(G_A):
                kmc = _dot(sel, expand_ref[g, :, lo:hi])
                valid = (jnp.concatenate([kmc, kmc, kmc], axis=0) > 0.5) & causal
                sc = jnp.where(valid, _dot(q3s[g], kk) - slopes[g] * dist, NEG_BIG)
                s_ref[g, :, lo:hi] = sc
                m = mx_ref[g]
                for k in range(SEL_CHUNK // LANE):
                    m = jnp.maximum(m, sc[:, k * LANE:(k + 1) * LANE])
                mx_ref[g] = m

    row_max = [jnp.max(mx_ref[g], axis=-1, keepdims=True) for g in range(G_A)]
    for c in range(n_chunk_total):
        lo, hi = c * SEL_CHUNK, (c + 1) * SEL_CHUNK

        @pl.when(c < n_chunks)
        def _(lo=lo, hi=hi):
            vv = ksv_ref[D_KV_A:2 * D_KV_A, lo:hi].astype(BF16)
            for g in range(G_A):
                sc = s_ref[g, :, lo:hi]
                e = jnp.where(sc > 0.5 * NEG_BIG, jnp.exp(sc - row_max[g]), 0.0)
                part = l_ref[g]
                for k in range(SEL_CHUNK // LANE):
                    part = part + e[:, k * LANE:(k + 1) * LANE]
                l_ref[g] = part
                acc_ref[g] = acc_ref[g] + _dot_nt(e.astype(BF16), vv)

    kw = jnp.concatenate([w[0:D_KV_A, :] for w in win_refs], axis=1).astype(BF16)
    vw = jnp.concatenate([w[D_KV_A:2 * D_KV_A, :] for w in win_refs], axis=1).astype(BF16)
    tk = start - WINDOW + lax.broadcasted_iota(jnp.int32, (1, N_WIN_BLOCKS * Q_TILE), 1)
    wdist = tq3 - tk
    wmask = (wdist >= 0) & (wdist < WINDOW) & (tk >= 0)
    wdistf = wdist.astype(F32)

    outs = [jnp.zeros((Q_TILE, MXU_DIM), F32) for _ in range(R_A)]
    for g in range(G_A):
        q3, slope3 = q3s[g], slopes[g]
        o_s = acc_ref[g] / jnp.maximum(jnp.sum(l_ref[g], axis=-1, keepdims=True), 1e-30)
        p_w = _masked_softmax(_dot(q3, kw) - slope3 * wdistf, wmask)
        o_w = _dot_nt(p_w.astype(BF16), vw)

        in_group = (lane_out >= g * HEAD_DIM) & (lane_out < (g + 1) * HEAD_DIM)
        for r in range(R_A):
            col = 3 * (R_A * g + r)
            rs = slice(r * Q_TILE, (r + 1) * Q_TILE)
            merged = (gate_ref[:, col:col + 1] * o_cs[g][rs] + gate_ref[:, col + 1:col + 2] * o_s[rs]
                      + gate_ref[:, col + 2:col + 3] * o_w[rs])
            outs[r] = jnp.where(in_group, merged, outs[r])

    for r in range(R_A):
        out_ref[:, r * MXU_DIM:(r + 1) * MXU_DIM] = outs[r]


def _nsa_prompt(qa, gates, kc, vc, nsa_t, win_t, place, expand, b, t):
    nq = t // Q_TILE
    cmp_spec = pl.BlockSpec((None, kc.shape[1], D_KV_A), lambda i, j: (i, 0, 0))
    rows3 = R_A * Q_TILE

    def win_spec(c):
        return pl.BlockSpec((None, 2 * D_KV_A, Q_TILE),
                            lambda i, j: (i, 0, jnp.maximum(j - (N_WIN_BLOCKS - 1) + c, 0)))

    return pl.pallas_call(
        _nsa_prompt_kernel,
        grid=(b, nq),
        in_specs=[pl.BlockSpec((Q_TILE, D_A), lambda i, j: (i * nq + j, 0)),
                  pl.BlockSpec((Q_TILE, LANE), lambda i, j: (i * nq + j, 0)),
                  cmp_spec, cmp_spec,
                  pl.BlockSpec((None, 2 * D_KV_A, t), lambda i, j: (i, 1, 0))]
        + [win_spec(c) for c in range(N_WIN_BLOCKS)]
        + [_const_spec(place.shape), _const_spec(expand.shape)],
        out_specs=pl.BlockSpec((Q_TILE, D_A), lambda i, j: (i * nq + j, 0)),
        out_shape=jax.ShapeDtypeStruct((b * t, D_A), F32),
        scratch_shapes=[pltpu.VMEM((G_A, rows3, t), F32),
                        pltpu.VMEM((G_A, rows3, LANE), F32), pltpu.VMEM((G_A, rows3, LANE), F32),
                        pltpu.VMEM((G_A, rows3, MXU_DIM), F32)],
        compiler_params=_cparams(("arbitrary", "arbitrary")),
        name="nsa_prompt",
    )(qa, gates, kc, vc, nsa_t, *([win_t] * N_WIN_BLOCKS), place, expand)


SB_TILE = 256
SB_Q_TILE = 256
SB_GROUP = 4


def _sb_prompt_kernel(q_ref, kt_ref, vt_ref, u_ref, out_ref, carry_ref, acc_ref):
    qi = pl.program_id(2)
    n_chunk_total = kt_ref.shape[1] // SB_TILE
    lane = lax.broadcasted_iota(jnp.int32, (1, LANE), 1)
    q = q_ref[...] * QK_SCALE
    qh = [jnp.where(lane < HEAD_DIM, q, 0.0).astype(BF16), jnp.where(lane >= HEAD_DIM, q, 0.0).astype(BF16)]
    tq = qi * SB_Q_TILE + lax.broadcasted_iota(jnp.int32, (SB_Q_TILE, 1), 0)
    carry_ref[...] = jnp.zeros(carry_ref.shape, F32)
    acc_ref[...] = jnp.zeros(acc_ref.shape, F32)

    def chunks(js):
        u = u_ref[...]
        stage = []
        for j in js:
            lo, hi = j * SB_TILE, (j + 1) * SB_TILE
            kk = kt_ref[:, lo:hi].astype(BF16)
            vv = vt_ref[:, lo:hi].astype(BF16)
            tk = lo + lax.broadcasted_iota(jnp.int32, (1, SB_TILE), 1)
            valid = tk < tq
            for h in range(2):
                z = _dot(qh[h], kk)
                lf_all = _log_sigmoid_neg(z)
                lf = jnp.where(valid, lf_all, 0.0)
                stage.append((h, valid, vv, lf_all + z + _split_dot(lf, u), jnp.sum(lf, axis=-1, keepdims=True)))
        carry = [carry_ref[0], carry_ref[1]]
        acc = [acc_ref[0], acc_ref[1]]
        for h, valid, vv, logit, row_sum in stage:
            a = jnp.where(valid, jnp.exp(logit + carry[h]), 0.0)
            acc[h] = acc[h] + _dot_nt(a.astype(BF16), vv)
            carry[h] = carry[h] + row_sum
        for h in range(2):
            carry_ref[h] = carry[h]
            acc_ref[h] = acc[h]

    assert n_chunk_total % SB_GROUP == 0 and SB_Q_TILE == SB_TILE
    for p in reversed(range(n_chunk_total // SB_GROUP)):
        base = SB_GROUP * p

        @pl.when(base + SB_GROUP - 1 <= qi)
        def _(base=base):
            chunks(tuple(range(base + SB_GROUP - 1, base - 1, -1)))

        for n in range(1, SB_GROUP):
            @pl.when(base + n - 1 == qi)
            def _(base=base, n=n):
                chunks(tuple(range(base + n - 1, base - 1, -1)))

    out_ref[...] = jnp.where(lane < HEAD_DIM, acc_ref[0], acc_ref[1])


def _sb_prompt(qb, sb_t, umat, b, t):
    nq = t // SB_Q_TILE
    n_pair = D_B // LANE
    return pl.pallas_call(
        _sb_prompt_kernel,
        grid=(b, n_pair, nq),
        in_specs=[pl.BlockSpec((SB_Q_TILE, LANE), lambda i, h, j: (i * nq + j, h)),
                  pl.BlockSpec((None, LANE, t), lambda i, h, j: (i, h, 0)),
                  pl.BlockSpec((None, LANE, t), lambda i, h, j: (i, n_pair + h, 0)),
                  _const_spec((SB_TILE, SB_TILE))],
        out_specs=pl.BlockSpec((SB_Q_TILE, LANE), lambda i, h, j: (i * nq + j, h)),
        out_shape=jax.ShapeDtypeStruct((b * t, D_B), F32),
        scratch_shapes=[pltpu.VMEM((2, SB_Q_TILE, 1), F32), pltpu.VMEM((2, SB_Q_TILE, LANE), F32)],
        compiler_params=_cparams(("arbitrary", "arbitrary", "arbitrary")),
        name="sb_prompt",
    )(qb, sb_t, sb_t, umat)


CONV_TILE = 256
CONV_HALO = 32


def _conv_finish(y, cb_ref, lg_ref, lb_ref):
    y = y + cb_ref[...]
    mu = jnp.mean(y, axis=-1, keepdims=True)
    d = y - mu
    var = jnp.mean(d * d, axis=-1, keepdims=True)
    z = d * lax.rsqrt(var + EPS) * lg_ref[...] + lb_ref[...]
    return z * _sigmoid(z)


def _conv_prompt_kernel(cin_ref, halo_ref, cw_ref, cb_ref, lg_ref, lb_ref, out_ref, state_ref, ext_ref):
    ti = pl.program_id(1)
    c = cin_ref[...]
    u = c[:, 0:D_C] * _sigmoid(c[:, D_C:])
    hc = halo_ref[...]
    uh = hc[:, 0:D_C] * _sigmoid(hc[:, D_C:])
    ext_ref[0:CONV_HALO, :] = jnp.where(ti == 0, 0.0, uh)
    ext_ref[CONV_HALO:, :] = u
    off = CONV_HALO - (CONV_W - 1)
    y = jnp.zeros((CONV_TILE, D_C), F32)
    for w in range(CONV_W):
        y = y + ext_ref[off + w:off + w + CONV_TILE, :] * cw_ref[w:w + 1, :]
    out_ref[...] = _conv_finish(y, cb_ref, lg_ref, lb_ref)
    state_ref[...] = ext_ref[CONV_HALO + CONV_TILE - (CONV_W - 1):, :]


def _conv_prompt(cin, cw, cb, lg, lb, b, t):
    nt = t // CONV_TILE
    per = CONV_TILE // CONV_HALO
    return pl.pallas_call(
        _conv_prompt_kernel,
        grid=(b, nt),
        in_specs=[pl.BlockSpec((CONV_TILE, 2 * D_C), lambda i, j: (i * nt + j, 0)),
                  pl.BlockSpec((CONV_HALO, 2 * D_C), lambda i, j: (jnp.maximum((i * nt + j) * per - 1, 0), 0)),
                  _const_spec((CONV_W, D_C)), _const_spec((1, D_C)), _const_spec((1, D_C)), _const_spec((1, D_C))],
        out_specs=[pl.BlockSpec((CONV_TILE, D_C), lambda i, j: (i * nt + j, 0)),
                   pl.BlockSpec((None, CONV_W - 1, D_C), lambda i, j: (i, 0, 0))],
        out_shape=[jax.ShapeDtypeStruct((b * t, D_C), F32), jax.ShapeDtypeStruct((b, CONV_W - 1, D_C), F32)],
        scratch_shapes=[pltpu.VMEM((CONV_HALO + CONV_TILE, D_C), F32)],
        compiler_params=_cparams(("arbitrary", "arbitrary")),
        name="conv_prompt",
    )(cin, cin, cw, cb, lg, lb)


def _conv_sample_kernel(cin_ref, buf_ref, cw_ref, cb_ref, lg_ref, lb_ref, out_ref, state_ref):
    c = cin_ref[...]
    u = c[:, 0:D_C] * _sigmoid(c[:, D_C:])
    y = u * cw_ref[CONV_W - 1:CONV_W, :]
    for w in range(CONV_W - 1):
        y = y + buf_ref[w] * cw_ref[w:w + 1, :]
    out_ref[...] = _conv_finish(y, cb_ref, lg_ref, lb_ref)
    for w in range(CONV_W - 2):
        state_ref[w] = buf_ref[w + 1]
    state_ref[CONV_W - 2] = u


def _conv_sample(cin, buf_t, layer, cw, cb, lg, lb):
    n = cin.shape[0]
    state_spec = pl.BlockSpec((CONV_W - 1, n, D_C), lambda i: (0, 0, 0))
    return pl.pallas_call(
        _conv_sample_kernel,
        grid=(1,),
        in_specs=[_const_spec((n, 2 * D_C)),
                  pl.BlockSpec((None, CONV_W - 1, n, D_C), lambda i: (layer, 0, 0, 0)),
                  _const_spec((CONV_W, D_C)), _const_spec((1, D_C)), _const_spec((1, D_C)), _const_spec((1, D_C))],
        out_specs=[_const_spec((n, D_C)), state_spec],
        out_shape=[jax.ShapeDtypeStruct((n, D_C), F32), jax.ShapeDtypeStruct((CONV_W - 1, n, D_C), F32)],
        compiler_params=_cparams(("arbitrary",)),
        name="conv_sample",
    )(cin, buf_t, cw, cb, lg, lb)


ROWS_S = R_A * SUBLANE
NSA_PAGES_PER_STEP = 8


def _nsa_sample_kernel(pt_ref, *refs, past_len, page_size):
    npg = NSA_PAGES_PER_STEP
    page_refs = refs[:npg]
    (qa_ref, gate_ref, new_ref, winnew_ref, winnewt_ref, kc_ref, vc_ref, win_ref, e2_ref,
     out_ref, winout_ref, q_ref, sel_ref, m_ref, l_ref, acc_ref, oc_ref, ow_ref) = refs[npg:]
    b = pl.program_id(0)
    j = pl.program_id(1)
    n_steps = pl.num_programs(1)
    tq = past_len
    n_cmp = kc_ref.shape[0]
    n_sel = (past_len + 1 + SEL_BLOCK - 1) // SEL_BLOCK
    sel_w = sel_ref.shape[0] * LANE
    rowi = lax.broadcasted_iota(jnp.int32, (ROWS_S, 1), 0)
    g_of_row = rowi & (SUBLANE - 1)
    r_of_row = lax.shift_right_logical(rowi, 3)
    slope = jnp.zeros((ROWS_S, 1), F32)
    for g in range(G_A):
        for r in range(R_A):
            slope = jnp.where((g_of_row == g) & (r_of_row == r), float(SLOPES[g, r]), slope)
    lane256 = lax.broadcasted_iota(jnp.int32, (ROWS_S, MXU_DIM), 1)
    own_lanes = lax.shift_right_logical(lane256, 6) == g_of_row

    @pl.when(j == 0)
    def _():
        qrow = qa_ref[...]
        lane_w = lax.broadcasted_iota(jnp.int32, (ROWS_S, MXU_DIM), 1)
        qmat = jnp.zeros((ROWS_S, MXU_DIM), F32)
        for g in range(G_A):
            for r in range(R_A):
                h = R_A * g + r
                blk = qrow[:, (h // 2) * LANE:(h // 2 + 1) * LANE]
                shift = (g * HEAD_DIM - (h % 2) * HEAD_DIM) % MXU_DIM
                wide = jnp.concatenate([blk, jnp.zeros((1, LANE), F32)], axis=-1)
                placed = pltpu.roll(wide, shift, axis=1) if shift else wide
                keep = (rowi == r * SUBLANE + g) & (lane_w >= g * HEAD_DIM) & (lane_w < (g + 1) * HEAD_DIM)
                qmat = jnp.where(keep, placed, qmat)
        q_ref[...] = (qmat * QK_SCALE).astype(BF16)
        q = q_ref[...]

        s = _dot_nt(q, kc_ref[...].astype(BF16))
        ends = lax.broadcasted_iota(jnp.int32, (1, n_cmp), 1) * CMP_STRIDE + (CMP_LEN - 1)
        s = s - slope * (float(tq) - ends.astype(F32))
        p_c = _masked_softmax(s, ends <= tq)
        oc_ref[...] = _dot(p_c.astype(BF16), vc_ref[...].astype(BF16))
        imp = p_c[0:SUBLANE] + p_c[SUBLANE:2 * SUBLANE] + p_c[2 * SUBLANE:3 * SUBLANE]
        imp = jnp.concatenate([imp, jnp.zeros((SUBLANE, sel_w - n_cmp), F32)], axis=-1)
        tq_col = jnp.full((SUBLANE, 1), tq, jnp.int32)
        sel = _select_blocks(_block_scores(imp, tq_col, n_sel), min(TOP_N, n_sel))
        for i in range(sel_w // LANE):
            sel_ref[i] = sel[:, i * LANE:(i + 1) * LANE]

        wb = win_ref.shape[1]
        kw = win_ref[0:D_KV_A, :].astype(BF16)
        vw = win_ref[D_KV_A:, :].astype(BF16)
        tk = (past_len - wb) + lax.broadcasted_iota(jnp.int32, (1, wb), 1)
        dist = tq - tk
        sw = _dot(q, kw) - slope * dist.astype(F32)
        mw = (dist >= 0) & (dist < WINDOW) & (tk >= 0)
        knew = winnew_ref[:, 0:D_KV_A].astype(BF16).astype(F32)
        vnew = winnew_ref[:, D_KV_A:].astype(BF16).astype(F32)
        s_new = jnp.sum(q.astype(F32) * knew, axis=-1, keepdims=True)
        sw = jnp.where(mw, sw, -jnp.inf)
        mx = jnp.maximum(jnp.max(sw, axis=-1, keepdims=True), s_new)
        e = jnp.where(mw, jnp.exp(sw - mx), 0.0)
        e_new = jnp.exp(s_new - mx)
        den = jnp.maximum(jnp.sum(e, axis=-1, keepdims=True) + e_new, 1e-30)
        p_w = e / den
        p_new = (e_new / den).astype(BF16).astype(F32)
        ow_ref[...] = _dot_nt(p_w.astype(BF16), vw) + p_new * vnew
        seq_lane = lax.broadcasted_iota(jnp.int32, winnewt_ref.shape, 1)
        new_col = jnp.sum(jnp.where(seq_lane == b, winnewt_ref[...], 0.0), axis=-1, keepdims=True)
        pos_lane = lax.broadcasted_iota(jnp.int32, (1, wb), 1)
        shifted = pltpu.roll(win_ref[...], wb - 1, axis=1)
        winout_ref[...] = jnp.where(pos_lane == wb - 1, new_col, shifted)

        m_ref[...] = jnp.full((ROWS_S, 1), NEG_BIG, F32)
        l_ref[...] = jnp.zeros((ROWS_S, 1), F32)
        acc_ref[...] = jnp.zeros((ROWS_S, MXU_DIM), F32)

    q = q_ref[...]
    step_keys = npg * page_size
    sel_tile = sel_ref[lax.shift_right_logical(j, 1)].astype(BF16)
    kmask = _dot(sel_tile, e2_ref[j & 1])
    kk = jnp.concatenate([pr[0:D_KV_A, :] for pr in page_refs], axis=1).astype(BF16)
    vv = jnp.concatenate([pr[D_KV_A:, :] for pr in page_refs], axis=1).astype(BF16)
    sc = _dot(q, kk)
    pos = j * step_keys + lax.broadcasted_iota(jnp.int32, (1, step_keys), 1)
    valid = (jnp.concatenate([kmask] * R_A, axis=0) > 0.5) & (pos <= tq)
    sc = jnp.where(valid, sc - slope * (float(tq) - pos.astype(F32)), NEG_BIG)
    m_old = m_ref[...]
    m_new = jnp.maximum(m_old, jnp.max(sc, axis=-1, keepdims=True))
    e = jnp.where(valid, jnp.exp(sc - m_new), 0.0)
    alpha = jnp.exp(m_old - m_new)
    l_ref[...] = alpha * l_ref[...] + jnp.sum(e, axis=-1, keepdims=True)
    acc_ref[...] = alpha * acc_ref[...] + _dot_nt(e.astype(BF16), vv)
    m_ref[...] = m_new

    @pl.when(j == n_steps - 1)
    def _():
        knew = new_ref[:, 2 * D_KV_A:3 * D_KV_A].astype(BF16).astype(F32)
        vnew = new_ref[:, 3 * D_KV_A:].astype(BF16).astype(F32)
        lane_new = 4 * (tq // SEL_BLOCK)
        sel_new = sel_ref[lane_new // LANE][:, lane_new % LANE:lane_new % LANE + 1]
        ok = jnp.concatenate([sel_new] * R_A, axis=0) > 0.5
        s_new = jnp.where(ok, jnp.sum(q.astype(F32) * knew, axis=-1, keepdims=True), NEG_BIG)
        m_old = m_ref[...]
        m_new = jnp.maximum(m_old, s_new)
        e_new = jnp.where(ok, jnp.exp(s_new - m_new), 0.0)
        alpha = jnp.exp(m_old - m_new)
        l_fin = alpha * l_ref[...] + e_new
        acc = alpha * acc_ref[...] + e_new.astype(BF16).astype(F32) * vnew
        o_s = acc / jnp.maximum(l_fin, 1e-30)

        grow = gate_ref[...]
        lane_g = lax.broadcasted_iota(jnp.int32, (ROWS_S, LANE), 1)
        col0 = 3 * (R_A * g_of_row + r_of_row)
        gsel = lambda c: jnp.sum(jnp.where(lane_g == col0 + c, grow, 0.0), axis=-1, keepdims=True)
        merged = gsel(0) * oc_ref[...] + gsel(1) * o_s + gsel(2) * ow_ref[...]
        merged = jnp.where(own_lanes & (g_of_row < G_A), merged, 0.0)
        for r in range(R_A):
            out_ref[:, r * MXU_DIM:(r + 1) * MXU_DIM] = jnp.sum(
                merged[r * SUBLANE:(r + 1) * SUBLANE], axis=0, keepdims=True)


def _nsa_sample(page_table, cache_t, layer, qa, gates, nsa_new, win_new, win_new_t, kc, vc, win_t, past_len):
    n_seq, n_pages = page_table.shape
    page_size = cache_t.shape[-1]
    npg = NSA_PAGES_PER_STEP
    n_steps = n_pages // npg
    wb = win_t.shape[-1]
    n_sel = (past_len + 1 + SEL_BLOCK - 1) // SEL_BLOCK
    sel_w = -(-(4 * n_sel) // LANE) * LANE
    tok = lambda w: pl.BlockSpec((None, 1, w), lambda b, j, pt: (b, 0, 0))

    def page_spec(k):
        return pl.BlockSpec((None, None, 2 * D_KV_A, page_size), lambda b, j, pt: (layer, pt[b, j * npg + k], 1, 0))

    cmp_spec = pl.BlockSpec((None, kc.shape[1], D_KV_A), lambda b, j, pt: (b, 0, 0))
    win_in_spec = pl.BlockSpec((None, None, 2 * D_KV_A, wb), lambda b, j, pt: (layer, b, 0, 0))
    win_out_spec = pl.BlockSpec((None, 2 * D_KV_A, wb), lambda b, j, pt: (b, 0, 0))
    newt_spec = pl.BlockSpec((2 * D_KV_A, n_seq), lambda b, j, pt: (0, 0))
    step_keys = npg * page_size
    lanes_per_step = 4 * step_keys // SEL_BLOCK
    assert 2 * lanes_per_step == LANE
    e2_np = np.zeros((2, LANE, step_keys), np.float32)
    for v in range(2):
        for k in range(step_keys):
            e2_np[v, v * lanes_per_step + 4 * (k // SEL_BLOCK), k] = 1.0
    e2 = jnp.asarray(e2_np, BF16)
    e2_spec = pl.BlockSpec(e2.shape, lambda b, j, pt: (0, 0, 0))
    grid_spec = pltpu.PrefetchScalarGridSpec(
        num_scalar_prefetch=1,
        grid=(n_seq, n_steps),
        in_specs=[page_spec(k) for k in range(npg)]
        + [tok(D_A), tok(LANE), tok(4 * D_KV_A), tok(2 * D_KV_A), newt_spec, cmp_spec, cmp_spec, win_in_spec, e2_spec],
        out_specs=[tok(D_A), win_out_spec],
        scratch_shapes=[pltpu.VMEM((ROWS_S, MXU_DIM), BF16), pltpu.VMEM((sel_w // LANE, SUBLANE, LANE), F32),
                        pltpu.VMEM((ROWS_S, 1), F32), pltpu.VMEM((ROWS_S, 1), F32),
                        pltpu.VMEM((ROWS_S, MXU_DIM), F32), pltpu.VMEM((ROWS_S, MXU_DIM), F32),
                        pltpu.VMEM((ROWS_S, MXU_DIM), F32)],
    )
    r3 = lambda a: a.reshape(n_seq, 1, a.shape[-1])
    oa, win_out = pl.pallas_call(
        functools.partial(_nsa_sample_kernel, past_len=past_len, page_size=page_size),
        grid_spec=grid_spec,
        out_shape=[jax.ShapeDtypeStruct((n_seq, 1, D_A), F32), jax.ShapeDtypeStruct((n_seq, 2 * D_KV_A, wb), F32)],
        compiler_params=_cparams(("arbitrary", "arbitrary")),
        name="nsa_sample",
    )(page_table, *([cache_t] * npg), r3(qa), r3(gates), r3(nsa_new), r3(win_new), win_new_t, kc, vc, win_t, e2)
    return oa.reshape(n_seq, D_A), win_out


ROWS_B = 16
SB_PAGES_PER_STEP = 8


def _sb_sample_kernel(pt_ref, *refs, past_len, page_size, n_pages):
    npg = SB_PAGES_PER_STEP
    page_refs = refs[:npg]
    q_ref, new_ref, u_ref, out_ref, qm_ref, carry_ref, acc_ref = refs[npg:]
    j = pl.program_id(1)
    n_steps = pl.num_programs(1)
    tq = past_len
    rowi = lax.broadcasted_iota(jnp.int32, (ROWS_B, D_B), 0)
    lane = lax.broadcasted_iota(jnp.int32, (ROWS_B, D_B), 1)
    own = lax.shift_right_logical(lane, 6) == rowi

    @pl.when(j == 0)
    def _():
        qm = jnp.where(own, q_ref[...] * QK_SCALE, 0.0)
        qm_ref[...] = qm.astype(BF16)
        acc_ref[...] = jnp.zeros(acc_ref.shape, F32)
        knew = new_ref[:, 0:D_B].astype(BF16).astype(F32)
        z = jnp.sum(qm.astype(BF16).astype(F32) * knew, axis=-1, keepdims=True)
        valid = jnp.full((ROWS_B, 1), past_len < tq)
        lf = jnp.where(valid, _log_sigmoid_neg(z), 0.0)
        a = jnp.where(valid, jnp.exp(_log_sigmoid_neg(z) + z), 0.0)
        acc_ref[...] = a.astype(BF16).astype(F32) * new_ref[:, D_B:].astype(BF16).astype(F32)
        carry_ref[...] = lf

    qm = qm_ref[...]
    u = u_ref[...]
    for c in range(npg // 2):
        newer, older = page_refs[2 * c], page_refs[2 * c + 1]
        pg_old = (n_pages - 1) - (j * npg + 2 * c + 1)
        kk = jnp.concatenate([older[0:D_B, :], newer[0:D_B, :]], axis=1).astype(BF16)
        vv = jnp.concatenate([older[D_B:, :], newer[D_B:, :]], axis=1).astype(BF16)
        tk = pg_old * page_size + lax.broadcasted_iota(jnp.int32, (1, 2 * page_size), 1)
        valid = tk < tq
        z = _dot(qm, kk)
        lf_all = _log_sigmoid_neg(z)
        lf = jnp.where(valid, lf_all, 0.0)
        later = carry_ref[...] + _split_dot(lf, u)
        a = jnp.where(valid, jnp.exp(lf_all + z + later), 0.0)
        acc_ref[...] = acc_ref[...] + _dot_nt(a.astype(BF16), vv)
        carry_ref[...] = carry_ref[...] + jnp.sum(lf, axis=-1, keepdims=True)

    @pl.when(j == n_steps - 1)
    def _():
        out_ref[...] = jnp.sum(jnp.where(own, acc_ref[...], 0.0), axis=0, keepdims=True)


def _sb_sample(page_table, cache_t, layer, qb, sb_new, umat, past_len):
    n_seq, n_pages = page_table.shape
    page_size = cache_t.shape[-1]
    npg = SB_PAGES_PER_STEP
    n_steps = n_pages // npg
    tok = lambda w: pl.BlockSpec((None, 1, w), lambda b, j, pt: (b, 0, 0))

    def page_spec(k):
        return pl.BlockSpec((None, None, 2 * D_B, page_size),
                            lambda b, j, pt: (layer, pt[b, n_pages - 1 - (j * npg + k)], 0, 0))

    grid_spec = pltpu.PrefetchScalarGridSpec(
        num_scalar_prefetch=1,
        grid=(n_seq, n_steps),
        in_specs=[page_spec(k) for k in range(npg)]
        + [tok(D_B), tok(2 * D_B), pl.BlockSpec((2 * page_size, 2 * page_size), lambda b, j, pt: (0, 0))],
        out_specs=tok(D_B),
        scratch_shapes=[pltpu.VMEM((ROWS_B, D_B), BF16), pltpu.VMEM((ROWS_B, 1), F32), pltpu.VMEM((ROWS_B, D_B), F32)],
    )
    r3 = lambda a: a.reshape(n_seq, 1, a.shape[-1])
    out = pl.pallas_call(
        functools.partial(_sb_sample_kernel, past_len=past_len, page_size=page_size, n_pages=n_pages),
        grid_spec=grid_spec,
        out_shape=jax.ShapeDtypeStruct((n_seq, 1, D_B), F32),
        compiler_params=_cparams(("arbitrary", "arbitrary")),
        name="sb_sample",
    )(page_table, *([cache_t] * npg), r3(qb), r3(sb_new), umat)
    return out.reshape(n_seq, D_B)


def _oa_perm():
    perm = np.zeros((D_A,), np.int32)
    for r in range(R_A):
        for g in range(G_A):
            for d in range(HEAD_DIM):
                perm[r * MXU_DIM + g * HEAD_DIM + d] = g * R_A * HEAD_DIM + r * HEAD_DIM + d
    return perm


def _place_mats():
    place = np.zeros((H_A, LANE, MXU_DIM), np.float32)
    for g in range(G_A):
        for r in range(R_A):
            h = R_A * g + r
            for d in range(HEAD_DIM):
                place[h, (h % 2) * HEAD_DIM + d, g * HEAD_DIM + d] = 1.0
    return place


def _expand_mat(t):
    e = np.zeros((G_A, Q_TILE, t), np.float32)
    for g in range(G_A):
        for k in range(t):
            e[g, 4 * (k // SEL_BLOCK) + g, k] = 1.0
    return e


def _prep_layer(l, w_in, w_out, norm_mix, norm_ffn, q_norm, k_norm, cmp_pos, cmp_w1, cmp_w2,
                out_norm_a, out_norm_b, conv_w, conv_b, conv_ln_g, conv_ln_b, w_gate, w_up, w_down):
    wi = w_in[l]
    w_tok = jnp.concatenate(
        [wi[:, :S_KV], wi[:, S_QB:S_KB], wi[:, S_CIN:], wi[:, S_GATE:S_QB], jnp.zeros((D_MODEL, LANE - N_GATE), F32),
         wi[:, S_KV:S_KV + 2 * D_KV_A]], axis=1).astype(BF16)
    wt = jnp.transpose(wi)
    w_feat = jnp.concatenate([wt[S_KV:S_GATE], wt[S_KB:S_CIN]], axis=0).astype(BF16)
    perm = _oa_perm()
    wo = w_out[l]
    w_out_p = jnp.concatenate([wo[:D_A][perm], wo[D_A:]], axis=0).astype(BF16)
    eye = jnp.eye(G_A, dtype=F32)
    w1 = cmp_w1[l].reshape(2, CMP_LEN, HEAD_DIM, HEAD_DIM)
    w1bd = jnp.einsum('gh,kpde->kpgdhe', eye, w1).reshape(2, 2, CMP_STRIDE * D_KV_A, D_KV_A).astype(BF16)
    w2bd = jnp.einsum('gh,kde->kgdhe', eye, cmp_w2[l]).reshape(2, D_KV_A, D_KV_A).astype(BF16)
    tile4 = lambda v: jnp.tile(v, G_A).reshape(1, D_KV_A)
    col4 = lambda v: jnp.tile(v, G_A).reshape(D_KV_A, 1)
    return dict(
        w_tok=w_tok, w_feat=w_feat, w_out_p=w_out_p, w1bd=w1bd, w2bd=w2bd,
        pos4=jnp.tile(cmp_pos[l], (1, 1, G_A)),
        gn_mix=norm_mix[l].reshape(1, D_MODEL), gn_ffn=norm_ffn[l].reshape(1, D_MODEL),
        qn4=tile4(q_norm[l]), kn0_4=tile4(k_norm[l, 0]), kn1_4=tile4(k_norm[l, 1]), kn2_4=tile4(k_norm[l, 2]),
        kn1c=col4(k_norm[l, 1]), kn2c=col4(k_norm[l, 2]),
        ga=out_norm_a[l][perm].reshape(1, D_A), gb=out_norm_b[l].reshape(1, D_B),
        cw=conv_w[l], cb=conv_b[l].reshape(1, D_C), lg=conv_ln_g[l].reshape(1, D_C), lb=conv_ln_b[l].reshape(1, D_C),
        wg=w_gate[l].astype(BF16), wu=w_up[l].astype(BF16), wd=w_down[l].astype(BF16),
    )


def kernel(x_prompt, x_sample, cache_nsa, cache_sb, state_win, state_conv, page_table, w_in, w_out, norm_mix,
           norm_ffn, q_norm, k_norm, cmp_pos, cmp_w1, cmp_w2, out_norm_a, out_norm_b, conv_w, conv_b, conv_ln_g,
           conv_ln_b, w_gate, w_up, w_down):
    b, t, _ = x_prompt.shape
    n_seq, dec_seq, _ = x_sample.shape
    assert dec_seq == 1 and t % SB_TILE == 0 and t % (CMP_STRIDE * SUBLANE) == 0 and t >= WINDOW
    n_pool, page_size = cache_nsa.shape[1], cache_nsa.shape[2]
    n_pages = page_table.shape[1]
    past_len = n_pages * page_size
    assert page_size % CMP_STRIDE == 0 and page_size // CMP_STRIDE == SUBLANE
    assert n_pages % CMP_PAGES_PER_STEP == 0 and n_pages % NSA_PAGES_PER_STEP == 0 and n_pages % SB_PAGES_PER_STEP == 0

    pmat = jnp.asarray(np.kron(np.eye(G_A, dtype=np.float32), np.full((HEAD_DIM, HEAD_DIM), 1.0 / HEAD_DIM, np.float32)), BF16)
    place = jnp.asarray(_place_mats(), BF16)
    expand = jnp.asarray(_expand_mat(t), BF16)
    u_prompt = jnp.asarray(np.tril(np.ones((SB_TILE, SB_TILE), np.float32), -1), BF16)
    u_page = jnp.asarray(np.tril(np.ones((2 * page_size, 2 * page_size), np.float32), -1), BF16)

    fmaj = lambda a: jnp.transpose(a, (0, 1, 3, 4, 5, 2))
    cache_nsa_t = fmaj(cache_nsa).reshape(DEPTH, n_pool, 4 * D_KV_A, page_size)
    cache_sb_t = fmaj(cache_sb).reshape(DEPTH, n_pool, 2 * D_B, page_size)
    wb = state_win.shape[2]
    win_t = fmaj(state_win).reshape(DEPTH, n_seq, 2 * D_KV_A, wb)
    conv_t = jnp.transpose(state_conv, (0, 2, 1, 3))
    tmaj = lambda a, lead: jnp.transpose(a.reshape(a.shape[0], *lead, a.shape[-1]), (0, len(lead) + 1, *range(1, len(lead) + 1)))

    xp = x_prompt.reshape(b * t, D_MODEL)
    xs = x_sample.reshape(n_seq, D_MODEL)
    outs = {k: [] for k in ("p_nsa", "p_sb", "p_win", "p_conv", "s_nsa", "s_sb", "s_win", "s_conv")}
    for l in range(DEPTH):
        w = _prep_layer(l, w_in, w_out, norm_mix, norm_ffn, q_norm, k_norm, cmp_pos, cmp_w1, cmp_w2, out_norm_a,
                        out_norm_b, conv_w, conv_b, conv_ln_g, conv_ln_b, w_gate, w_up, w_down)

        qa, nsa_t, winp_t, qb, sb_t, cin, gates, cmp_rows = _in_proj_prompt(
            xp, w["gn_mix"], w["w_tok"], w["w_feat"], w["qn4"], w["kn1c"], w["kn2c"], pmat, b, t)
        kc, vc = _compress_prompt(cmp_rows, w["pos4"], w["w1bd"], w["w2bd"], w["kn0_4"], pmat, b, t)
        oa = _nsa_prompt(qa, gates, kc, vc, nsa_t, winp_t, place, expand, b, t)
        ob = _sb_prompt(qb, sb_t, u_prompt, b, t)
        cv, conv_state = _conv_prompt(cin, w["cw"], w["cb"], w["lg"], w["lb"], b, t)
        xp = _out_proj(oa, ob, cv, xp, w["ga"], w["gb"], w["w_out_p"], tm=512)
        xp = _ffn(xp, w["gn_ffn"], w["wg"], w["wu"], w["wd"], tm=1024, tf=512, tn=512)
        outs["p_nsa"].append(tmaj(nsa_t, (4, G_A, HEAD_DIM)))
        outs["p_sb"].append(tmaj(sb_t, (2, H_B, HEAD_DIM)))
        outs["p_win"].append(tmaj(winp_t[:, :, t - min(WINDOW, t):], (2, G_A, HEAD_DIM)))
        outs["p_conv"].append(conv_state)

        qa, nsa_new, win_new, qb, sb_new, cin, gates, win_new_t = _in_proj_sample(
            xs, w["gn_mix"], w["w_tok"], w["w_feat"], w["qn4"], w["kn1_4"], w["kn2_4"], w["kn2c"], pmat)
        kc, vc = _compress_sample(page_table, cache_nsa_t, l, w["pos4"], w["w1bd"], w["w2bd"], w["kn0_4"], pmat)
        oa, win_out_t = _nsa_sample(page_table, cache_nsa_t, l, qa, gates, nsa_new, win_new, win_new_t, kc, vc,
                                    win_t, past_len)
        ob = _sb_sample(page_table, cache_sb_t, l, qb, sb_new, u_page, past_len)
        cv, conv_state_t = _conv_sample(cin, conv_t, l, w["cw"], w["cb"], w["lg"], w["lb"])
        xs = _out_proj(oa, ob, cv, xs, w["ga"], w["gb"], w["w_out_p"], tm=n_seq)
        xs = _ffn(xs, w["gn_ffn"], w["wg"], w["wu"], w["wd"], tm=n_seq, tf=512, tn=512)
        outs["s_nsa"].append(nsa_new.reshape(n_seq, 1, 4, G_A, HEAD_DIM))
        outs["s_sb"].append(sb_new.reshape(n_seq, 1, 2, H_B, HEAD_DIM))
        outs["s_win"].append(tmaj(win_out_t, (2, G_A, HEAD_DIM)))
        outs["s_conv"].append(jnp.transpose(conv_state_t, (1, 0, 2)))

    st = lambda k: jnp.stack(outs[k])
    return (xp.reshape(b, t, D_MODEL), xs.reshape(n_seq, 1, D_MODEL), st("p_nsa"), st("p_sb"), st("p_win"),
            st("p_conv"), st("s_nsa"), st("s_sb"), st("s_win"), st("s_conv"))
```
